```python
import jax, jax.numpy as jnp
from jax import lax
import numpy as np

D_MODEL = 2048
BATCH = 2
SEQ = 8192
DEPTH = 1

NORM_EPS = 1e-6
ROPE_THETA = 10000.0
GLA_HEADS = 4
GLA_DK = 256
GLA_DV = 512
GLA_GATE_RANK = 16
GLA_GATE_NORMALIZER = 16.0
GLA_CHUNK = 64
DSA_HEADS = 16
DSA_KV_HEADS = 4
DSA_HEAD_DIM = 128
IDX_HEADS = 16
IDX_DIM = 128
IDX_ROPE_DIM = 64
TOPK_MAX = 256
Q_BLOCK = 128
D_FF = 4 * D_MODEL
IN_SPLITS = (
    GLA_HEADS * GLA_DK,
    GLA_HEADS * GLA_DK,
    GLA_HEADS * GLA_DV,
    GLA_HEADS * GLA_DV,
    GLA_GATE_RANK,
    DSA_HEADS * DSA_HEAD_DIM,
    DSA_KV_HEADS * DSA_HEAD_DIM,
    DSA_KV_HEADS * DSA_HEAD_DIM,
    IDX_HEADS * IDX_DIM,
    IDX_DIM,
    IDX_HEADS,
    2 * D_MODEL,
)
D_IN_PROJ = sum(IN_SPLITS)

kernel_name = 'hybrid_gla_dsa_block'


def rmsnorm(x, g):
    xf = x.astype(jnp.float32)
    y = xf * lax.rsqrt(jnp.mean(xf * xf, axis=-1, keepdims=True) + NORM_EPS)
    return (y * g.astype(jnp.float32)).astype(x.dtype)


def rope(x, pos, rot_dim):
    half = rot_dim // 2
    inv_freq = ROPE_THETA ** (-jnp.arange(half, dtype=jnp.float32) * 2.0 / rot_dim)
    ang = pos.astype(jnp.float32)[:, None] * inv_freq[None, :]
    cos = jnp.cos(ang)[None, :, None, :]
    sin = jnp.sin(ang)[None, :, None, :]
    xf = x.astype(jnp.float32)
    x1 = xf[..., :half]
    x2 = xf[..., half:rot_dim]
    out = jnp.concatenate([x1 * cos - x2 * sin, x2 * cos + x1 * sin, xf[..., rot_dim:]], axis=-1)
    return out.astype(x.dtype)


def gla_chunked(q, k, v, log_a):
    B, S, H, dk = q.shape
    dv = v.shape[-1]
    C = GLA_CHUNK
    n = S // C

    def chunks(t):
        return t.astype(jnp.float32).reshape(B, n, C, H, t.shape[-1]).transpose(1, 0, 3, 2, 4)

    qc = chunks(q) * (dk ** -0.5)
    kc, vc, gc = chunks(k), chunks(v), chunks(log_a)
    causal = jnp.tril(jnp.ones((C, C), dtype=bool))[:, :, None]

    def step(state, inp):
        qi, ki, vi, gi = inp
        b = jnp.cumsum(gi, axis=2)
        o_inter = jnp.einsum('bhcd,bhde->bhce', qi * jnp.exp(b), state)
        diff = b[:, :, :, None, :] - b[:, :, None, :, :]
        decay = jnp.exp(jnp.where(causal, diff, -jnp.inf))
        scores = jnp.einsum('bhid,bhjd,bhijd->bhij', qi, ki, decay)
        o_intra = jnp.einsum('bhij,bhje->bhie', scores, vi)
        b_last = b[:, :, -1:, :]
        state = state * jnp.exp(b_last[:, :, 0, :])[..., None] + jnp.einsum('bhcd,bhce->bhde', ki * jnp.exp(b_last - b), vi)
        return state, o_inter + o_intra

    state0 = jnp.zeros((B, H, dk, dv), jnp.float32)
    _, out = lax.scan(step, state0, (qc, kc, vc, gc))
    return out.transpose(1, 0, 3, 2, 4).reshape(B, S, H, dv).astype(v.dtype)


def dsa_sparse_attention(q, k, v, q_idx, k_idx, w_idx, topk):
    B, S, H, hd = q.shape
    hkv = k.shape[2]
    grp = H // hkv
    nb = S // Q_BLOCK
    key_pos = jnp.arange(S, dtype=jnp.int32)
    q_pos = key_pos.reshape(nb, Q_BLOCK)
    gather_rows = jax.vmap(lambda t, i: t[i])
    k_idx_f = k_idx.astype(jnp.float32)

    def blocks(t):
        return t.reshape(B, nb, Q_BLOCK, *t.shape[2:]).swapaxes(0, 1)

    def one_block(args):
        qb, qib, wb, qp = args
        rel = jax.nn.relu(jnp.einsum('bthd,bsd->bths', qib.astype(jnp.float32), k_idx_f))
        score = jnp.einsum('bth,bths->bts', wb.astype(jnp.float32), rel)
        score = jnp.where(key_pos[None, None, :] <= qp[None, :, None], score, -jnp.inf)
        _, sel = lax.top_k(score, topk)
        valid = sel <= qp[None, :, None]
        ks = gather_rows(k, sel).astype(jnp.float32)
        vs = gather_rows(v, sel).astype(jnp.float32)
        qg = qb.astype(jnp.float32).reshape(B, Q_BLOCK, hkv, grp, hd)
        logits = jnp.einsum('btkgd,btskd->btkgs', qg, ks) * (hd ** -0.5)
        logits = jnp.where(valid[:, :, None, None, :], logits, -jnp.inf)
        p = jax.nn.softmax(logits, axis=-1)
        o = jnp.einsum('btkgs,btskd->btkgd', p, vs)
        return o.reshape(B, Q_BLOCK, H, hd).astype(v.dtype)

    out = lax.map(one_block, (blocks(q), blocks(q_idx), blocks(w_idx), q_pos))
    return out.swapaxes(0, 1).reshape(B, S, H, hd)


def hybrid_block(x, norm1_g, w_in, gla_wg2, gla_bg, gla_norm_g, w_proj_gla, q_norm_g, k_norm_g,
                 idx_k_norm_g, w_proj_dsa, b_gate, w_out, norm2_g, w_ff1, w_ff2):
    B, S, _ = x.shape
    pos = jnp.arange(S, dtype=jnp.int32)
    topk = min(TOPK_MAX, S // 4)

    h = rmsnorm(x, norm1_g)
    proj = h @ w_in
    points = np.cumsum(IN_SPLITS)[:-1].tolist()
    gq, gk, gv, gr, glr, dq, dk, dv, iq, ik, iw, gates = jnp.split(proj, points, axis=-1)

    log_a = jax.nn.log_sigmoid((glr @ gla_wg2 + gla_bg).astype(jnp.float32)) / GLA_GATE_NORMALIZER
    o_gla = gla_chunked(gq.reshape(B, S, GLA_HEADS, GLA_DK), gk.reshape(B, S, GLA_HEADS, GLA_DK),
                        gv.reshape(B, S, GLA_HEADS, GLA_DV), log_a.reshape(B, S, GLA_HEADS, GLA_DK))
    o_gla = rmsnorm(o_gla, gla_norm_g) * jax.nn.silu(gr.reshape(B, S, GLA_HEADS, GLA_DV))
    y_gla = o_gla.reshape(B, S, GLA_HEADS * GLA_DV) @ w_proj_gla

    dq = rope(rmsnorm(dq.reshape(B, S, DSA_HEADS, DSA_HEAD_DIM), q_norm_g), pos, DSA_HEAD_DIM)
    dk = rope(rmsnorm(dk.reshape(B, S, DSA_KV_HEADS, DSA_HEAD_DIM), k_norm_g), pos, DSA_HEAD_DIM)
    dv = dv.reshape(B, S, DSA_KV_HEADS, DSA_HEAD_DIM)
    iq = rope(iq.reshape(B, S, IDX_HEADS, IDX_DIM), pos, IDX_ROPE_DIM) * (IDX_DIM ** -0.5)
    ik = rope(rmsnorm(ik, idx_k_norm_g)[:, :, None, :], pos, IDX_ROPE_DIM)[:, :, 0, :]
    iw = iw * (IDX_HEADS ** -0.5)
    o_dsa = dsa_sparse_attention(dq, dk, dv, iq, ik, iw, topk)
    y_dsa = o_dsa.reshape(B, S, DSA_HEADS * DSA_HEAD_DIM) @ w_proj_dsa

    g_gla, g_dsa = jnp.split(jax.nn.sigmoid(gates + b_gate), 2, axis=-1)
    x = x + (g_gla * y_gla + g_dsa * y_dsa) @ w_out

    h2 = rmsnorm(x, norm2_g)
    return x + jnp.square(jax.nn.relu(h2 @ w_ff1)) @ w_ff2


def setup_inputs(seed: int = 0) -> dict:
    key = jax.random.key(seed)
    ks = jax.random.split(key, 16)

    def nrm(k, shape, scale):
        return jax.random.normal(k, shape, jnp.float32) * scale

    def gain(k, n):
        return 1.0 + nrm(k, (DEPTH, n), 0.02)

    return {
        'x': nrm(ks[0], (BATCH, SEQ, D_MODEL), 1.0),
        'norm1_g': gain(ks[1], D_MODEL),
        'w_in': nrm(ks[2], (DEPTH, D_MODEL, D_IN_PROJ), D_MODEL ** -0.5),
        'gla_wg2': nrm(ks[3], (DEPTH, GLA_GATE_RANK, GLA_HEADS * GLA_DK), GLA_GATE_RANK ** -0.5),
        'gla_bg': nrm(ks[4], (DEPTH, GLA_HEADS * GLA_DK), 0.1),
        'gla_norm_g': gain(ks[5], GLA_DV),
        'w_proj_gla': nrm(ks[6], (DEPTH, GLA_HEADS * GLA_DV, D_MODEL), (GLA_HEADS * GLA_DV) ** -0.5),
        'q_norm_g': gain(ks[7], DSA_HEAD_DIM),
        'k_norm_g': gain(ks[8], DSA_HEAD_DIM),
        'idx_k_norm_g': gain(ks[9], IDX_DIM),
        'w_proj_dsa': nrm(ks[10], (DEPTH, DSA_HEADS * DSA_HEAD_DIM, D_MODEL), (DSA_HEADS * DSA_HEAD_DIM) ** -0.5),
        'b_gate': nrm(ks[11], (DEPTH, 2 * D_MODEL), 0.02),
        'w_out': nrm(ks[12], (DEPTH, D_MODEL, D_MODEL), D_MODEL ** -0.5),
        'norm2_g': gain(ks[13], D_MODEL),
        'w_ff1': nrm(ks[14], (DEPTH, D_MODEL, D_FF), D_MODEL ** -0.5),
        'w_ff2': nrm(ks[15], (DEPTH, D_FF, D_MODEL), D_FF ** -0.5),
    }


def reference(x, norm1_g, w_in, gla_wg2, gla_bg, gla_norm_g, w_proj_gla, q_norm_g, k_norm_g,
              idx_k_norm_g, w_proj_dsa, b_gate, w_out, norm2_g, w_ff1, w_ff2):
    for l in range(DEPTH):
        x = hybrid_block(x, norm1_g[l], w_in[l], gla_wg2[l], gla_bg[l], gla_norm_g[l], w_proj_gla[l],
                         q_norm_g[l], k_norm_g[l], idx_k_norm_g[l], w_proj_dsa[l], b_gate[l], w_out[l],
                         norm2_g[l], w_ff1[l], w_ff2[l])
    return x
```

```python
import functools
import math

import jax
import jax.numpy as jnp
from jax import lax
from jax.experimental import pallas as pl
from jax.experimental.pallas import tpu as pltpu

BF16 = jnp.bfloat16
F32 = jnp.float32

NORM_EPS = 1e-6
ROPE_THETA = 10000.0
GLA_HEADS = 4
GLA_DK = 256
GLA_DV = 512
GLA_GATE_RANK = 16
GLA_GATE_NORMALIZER = 16.0
DSA_HEADS = 16
DSA_KV_HEADS = 4
HEAD_DIM = 128
IDX_HEADS = 16
IDX_DIM = 128
IDX_ROPE_DIM = 64
TOPK_MAX = 256

LANES = 128
VMEM_LIMIT = 56 * 1024 * 1024
GLA_CHUNK = 128
GLA_SUB = 16
DSA_TQ = 256
DSA_TK = 512
INT_MIN = -(2 ** 31)
NEG_BIG = -1e30
LOG2E = math.log2(math.e)


def _params(*sem):
    return pltpu.CompilerParams(dimension_semantics=sem, vmem_limit_bytes=VMEM_LIMIT)


def _dot(a, b):
    return jnp.dot(a, b, preferred_element_type=F32)


def _rms_body(x_ref, g_ref, h_ref, ht_ref):
    x = x_ref[...]
    y = x * lax.rsqrt(jnp.mean(x * x, axis=-1, keepdims=True) + NORM_EPS) * g_ref[...]
    h_ref[...] = y.astype(BF16)
    ht_ref[...] = y.T.astype(BF16)


def _rmsnorm_both(x2, g, tm=256):
    t, d = x2.shape
    return pl.pallas_call(
        _rms_body,
        grid=(t // tm,),
        in_specs=[pl.BlockSpec((tm, d), lambda i: (i, 0)), pl.BlockSpec((1, d), lambda i: (0, 0))],
        out_specs=[pl.BlockSpec((tm, d), lambda i: (i, 0)), pl.BlockSpec((d, tm), lambda i: (0, i))],
        out_shape=[jax.ShapeDtypeStruct((t, d), BF16), jax.ShapeDtypeStruct((d, t), BF16)],
        compiler_params=_params("parallel"),
        name="rmsnorm_in",
    )(x2, g.reshape(1, d))


def _mm_body(a_ref, w_ref, o_ref):
    o_ref[...] = _dot(a_ref[...], w_ref[...]).astype(o_ref.dtype)


def _mm(a, w, out_dtype, tm, tn, name):
    m, k = a.shape
    n = w.shape[1]
    return pl.pallas_call(
        _mm_body,
        grid=(m // tm, n // tn),
        in_specs=[pl.BlockSpec((tm, k), lambda i, j: (i, 0)), pl.BlockSpec((k, tn), lambda i, j: (0, j))],
        out_specs=pl.BlockSpec((tm, tn), lambda i, j: (i, j)),
        out_shape=jax.ShapeDtypeStruct((m, n), out_dtype),
        compiler_params=_params("parallel", "parallel"),
        name=name,
    )(a, w)


def _projk_body(h_ref, w_ref, ka_ref, kb_ref, ia_ref, ib1_ref, ib2_ref, dk_ref, ik_ref, glr_ref):
    p = _dot(h_ref[...], w_ref[...])
    nkv = DSA_KV_HEADS
    for j in range(nkv):
        pj = p[:, j * HEAD_DIM:(j + 1) * HEAD_DIM]
        n = pj * lax.rsqrt(jnp.mean(pj * pj, axis=-1, keepdims=True) + NORM_EPS)
        out = n * ka_ref[...] + pltpu.roll(n, HEAD_DIM // 2, 1) * kb_ref[...]
        dk_ref[:, j * HEAD_DIM:(j + 1) * HEAD_DIM] = out.astype(BF16)
    c0 = nkv * HEAD_DIM
    pi = p[:, c0:c0 + IDX_DIM]
    n = pi * lax.rsqrt(jnp.mean(pi * pi, axis=-1, keepdims=True) + NORM_EPS)
    half = IDX_ROPE_DIM // 2
    out = n * ia_ref[...] + pltpu.roll(n, IDX_DIM - half, 1) * ib1_ref[...] + pltpu.roll(n, half, 1) * ib2_ref[...]
    ik_ref[...] = out.astype(BF16)
    glr_ref[...] = p[:, c0 + IDX_DIM:]


def _proj_k(h, w_k, tabs, seq, tm=512):
    t, d = h.shape
    nblk = seq // tm
    ncol = w_k.shape[1]
    tab_spec = pl.BlockSpec((tm, LANES), lambda i: (i % nblk, 0))
    return pl.pallas_call(
        _projk_body,
        grid=(t // tm,),
        in_specs=[pl.BlockSpec((tm, d), lambda i: (i, 0)), pl.BlockSpec((d, ncol), lambda i: (0, 0))] + [tab_spec] * 5,
        out_specs=[pl.BlockSpec((tm, DSA_KV_HEADS * HEAD_DIM), lambda i: (i, 0)),
                   pl.BlockSpec((tm, IDX_DIM), lambda i: (i, 0)),
                   pl.BlockSpec((tm, LANES), lambda i: (i, 0))],
        out_shape=[jax.ShapeDtypeStruct((t, DSA_KV_HEADS * HEAD_DIM), BF16),
                   jax.ShapeDtypeStruct((t, IDX_DIM), BF16),
                   jax.ShapeDtypeStruct((t, LANES), F32)],
        compiler_params=_params("parallel"),
        name="proj_k",
    )(h, w_k, *tabs)


def _projt_body(kind, w_ref, ht_ref, *rest):
    p = _dot(w_ref[...], ht_ref[...])
    o_ref = rest[-1]
    rows = p.shape[0]
    if kind == "v":
        o_ref[...] = p.astype(o_ref.dtype)
        return
    if kind == "iw":
        o_ref[...] = p * (IDX_HEADS ** -0.5)
        return
    ta_ref, tb_ref = rest[0], rest[1]
    for j in range(rows // HEAD_DIM):
        pj = p[j * HEAD_DIM:(j + 1) * HEAD_DIM]
        if kind == "dq":
            n = pj * lax.rsqrt(jnp.mean(pj * pj, axis=0, keepdims=True) + NORM_EPS)
            sw = jnp.concatenate([n[HEAD_DIM // 2:], n[:HEAD_DIM // 2]], axis=0)
        else:
            n = pj
            half = IDX_ROPE_DIM // 2
            sw = jnp.concatenate([n[half:2 * half], n[:half], n[2 * half:]], axis=0)
        o_ref[j * HEAD_DIM:(j + 1) * HEAD_DIM, :] = (n * ta_ref[...] + sw * tb_ref[...]).astype(o_ref.dtype)


def _proj_t(kind, w_t, ht, tabs, seq, out_dtype, tc, tt=512):
    c, d = w_t.shape
    t = ht.shape[1]
    nblk = seq // tt
    tab_spec = pl.BlockSpec((HEAD_DIM, tt), lambda i, j: (0, j % nblk))
    return pl.pallas_call(
        functools.partial(_projt_body, kind),
        grid=(c // tc, t // tt),
        in_specs=[pl.BlockSpec((tc, d), lambda i, j: (i, 0)), pl.BlockSpec((d, tt), lambda i, j: (0, j))]
        + [tab_spec] * len(tabs),
        out_specs=pl.BlockSpec((tc, tt), lambda i, j: (i, j)),
        out_shape=jax.ShapeDtypeStruct((c, t), out_dtype),
        compiler_params=_params("parallel", "parallel"),
        name="proj_t_" + kind,
    )(w_t, ht, *tabs)


def _gla_body(q_ref, k_ref, v_ref, gr_ref, glr_ref, wg2_ref, bg_ref, ng_ref, o_ref, st_ref, b_scr):
    C, SB = GLA_CHUNK, GLA_SUB

    @pl.when(pl.program_id(2) == 0)
    def _():
        st_ref[...] = jnp.zeros_like(st_ref)

    q = q_ref[...] * (GLA_DK ** -0.5)
    k = k_ref[...]
    vb = v_ref[...].astype(BF16)
    z = _dot(glr_ref[...].astype(BF16), wg2_ref[...]) + bg_ref[...]
    g = (jnp.minimum(z, 0.0) - jnp.log1p(jnp.exp(-jnp.abs(z)))) * (1.0 / GLA_GATE_NORMALIZER)

    ri = lax.broadcasted_iota(jnp.int32, (C, C), 0)
    ci = lax.broadcasted_iota(jnp.int32, (C, C), 1)
    tri = jnp.where(ci <= ri, 1.0, 0.0).astype(BF16)
    g1 = g.astype(BF16)
    r1 = g - g1.astype(F32)
    g2 = r1.astype(BF16)
    g3 = (r1 - g2.astype(F32)).astype(BF16)
    b = _dot(tri, g1) + _dot(tri, g2) + _dot(tri, g3)
    b_scr[...] = b
    kt = k.T
    bt = b.T

    o = _dot((q * jnp.exp(b)).astype(BF16), st_ref[...].astype(BF16))

    lane = lax.broadcasted_iota(jnp.int32, (SB, C), 1)
    rowi = lax.broadcasted_iota(jnp.int32, (SB, 1), 0)
    blocks = []
    for blk in range(C // SB):
        s = blk * SB
        q_i = q[s:s + SB]
        b_i = b[s:s + SB]
        a = jnp.zeros((SB, C), F32)
        for j in range(SB):
            r = s + j
            dec = jnp.where(rowi >= j, jnp.exp(b_i - b_scr[r:r + 1, :]), 0.0)
            col = jnp.sum(q_i * k_ref[r:r + 1, :] * dec, axis=-1, keepdims=True)
            a = jnp.where(lane == r, col, a)
        if blk > 0:
            q_s = q_i * jnp.exp(b_i - b_scr[s - 1:s, :])
            k_s = kt * jnp.exp(jnp.minimum(bt[:, s - 1:s] - bt, 0.0))
            a = jnp.where(lane < s, _dot(q_s.astype(BF16), k_s.astype(BF16)), a)
        blocks.append(a)
    a_full = jnp.concatenate(blocks, axis=0)
    o = o + _dot(a_full.astype(BF16), vb)

    b_last = bt[:, C - 1:C]
    k_dec = kt * jnp.exp(b_last - bt)
    st_ref[...] = st_ref[...] * jnp.exp(b_last) + _dot(k_dec.astype(BF16), vb)

    on = o * lax.rsqrt(jnp.mean(o * o, axis=-1, keepdims=True) + NORM_EPS) * ng_ref[...]
    gr = gr_ref[...]
    o_ref[...] = (on * gr * (1.0 / (1.0 + jnp.exp(-gr)))).astype(o_ref.dtype)


def _gla(p_nat, glr, wg2p, bg, ng, batch, seq):
    C = GLA_CHUNK
    nc = seq // C
    t = batch * seq
    kq = GLA_HEADS * GLA_DK // GLA_DK
    kv = 2 * GLA_HEADS * GLA_DK // GLA_DV
    row = lambda b, h, c: b * nc + c
    return pl.pallas_call(
        _gla_body,
        grid=(batch, GLA_HEADS, nc),
        in_specs=[
            pl.BlockSpec((C, GLA_DK), lambda b, h, c: (row(b, h, c), h)),
            pl.BlockSpec((C, GLA_DK), lambda b, h, c: (row(b, h, c), kq + h)),
            pl.BlockSpec((C, GLA_DV), lambda b, h, c: (row(b, h, c), kv + h)),
            pl.BlockSpec((C, GLA_DV), lambda b, h, c: (row(b, h, c), kv + GLA_HEADS + h)),
            pl.BlockSpec((C, LANES), lambda b, h, c: (row(b, h, c), 0)),
            pl.BlockSpec((LANES, GLA_DK), lambda b, h, c: (0, h)),
            pl.BlockSpec((1, GLA_DK), lambda b, h, c: (0, h)),
            pl.BlockSpec((1, GLA_DV), lambda b, h, c: (0, 0)),
        ],
        out_specs=pl.BlockSpec((C, GLA_DV), lambda b, h, c: (row(b, h, c), h)),
        out_shape=jax.ShapeDtypeStruct((t, GLA_HEADS * GLA_DV), BF16),
        scratch_shapes=[pltpu.VMEM((GLA_DK, GLA_DV), F32), pltpu.VMEM((C, GLA_DK), F32)],
        compiler_params=_params("parallel", "parallel", "arbitrary"),
        name="gla",
    )(p_nat, p_nat, p_nat, p_nat, glr, wg2p, bg, ng)


def _dsa_body(topk, iqt_ref, iwt_ref, dqt_ref, ik_ref, dk_ref, vt_ref, o_ref, sc_ref, m_ref, l_ref, acc_ref):
    TQ, TK = DSA_TQ, DSA_TK
    i = pl.program_id(1)
    nk = (i * TQ + TQ + TK - 1) // TK
    q_pos = i * TQ + lax.broadcasted_iota(jnp.int32, (TK, TQ), 1)
    k_off = lax.broadcasted_iota(jnp.int32, (TK, TQ), 0)

    def score_tile(j, carry):
        base = pl.multiple_of(j * TK, TK)
        ikt = ik_ref[pl.ds(base, TK), :]
        acc = jnp.zeros((TK, TQ), F32)
        for h in range(IDX_HEADS):
            r = _dot(ikt, iqt_ref[h * IDX_DIM:(h + 1) * IDX_DIM, :])
            acc = acc + jnp.maximum(r, 0.0) * iwt_ref[h:h + 1, :]
        bits = pltpu.bitcast(acc, jnp.int32)
        key = jnp.where(bits < 0, bits ^ jnp.int32(0x7FFFFFFF), bits)
        key = jnp.where(base + k_off <= q_pos, key, jnp.int32(INT_MIN))
        sc_ref[pl.ds(base, TK), :] = key
        return carry

    lax.fori_loop(0, nk, score_tile, 0)

    def count_ge(cand):
        def body(j, cnt):
            base = pl.multiple_of(j * TK, TK)
            hit = jnp.where(sc_ref[pl.ds(base, TK), :] >= cand, 1.0, 0.0)
            return cnt + jnp.sum(hit.reshape(TK // 8, 8, TQ), axis=0)
        cnt = lax.fori_loop(0, nk, body, jnp.zeros((8, TQ), F32))
        return jnp.sum(cnt, axis=0, keepdims=True)

    zero = jnp.zeros((1, TQ), jnp.int32)
    thr0 = jnp.where(count_ge(zero) >= topk, zero, jnp.int32(INT_MIN))

    def bit_step(it, thr):
        cand = thr | jnp.left_shift(jnp.int32(1), 30 - it)
        return jnp.where(count_ge(cand) >= topk, cand, thr)

    thr = lax.fori_loop(0, 31, bit_step, thr0)
    thr = jnp.maximum(thr, jnp.int32(INT_MIN + 1))

    m_ref[...] = jnp.full_like(m_ref, NEG_BIG)
    l_ref[...] = jnp.zeros_like(l_ref)
    acc_ref[...] = jnp.zeros_like(acc_ref)
    grp = DSA_HEADS // DSA_KV_HEADS

    def attn_tile(j, carry):
        base = pl.multiple_of(j * TK, TK)
        bias = jnp.where(sc_ref[pl.ds(base, TK), :] >= thr, 0.0, NEG_BIG)
        for g in range(DSA_KV_HEADS):
            kt = dk_ref[pl.ds(base, TK), g * HEAD_DIM:(g + 1) * HEAD_DIM]
            vt = vt_ref[g * HEAD_DIM:(g + 1) * HEAD_DIM, pl.ds(base, TK)]
            for hh in range(grp):
                h = g * grp + hh
                s = _dot(kt, dqt_ref[h * HEAD_DIM:(h + 1) * HEAD_DIM, :]) + bias
                m_old = m_ref[h:h + 1, :]
                m_new = jnp.maximum(m_old, jnp.max(s, axis=0, keepdims=True))
                p = jnp.exp2(s - m_new)
                alpha = jnp.exp2(m_old - m_new)
                l_ref[h:h + 1, :] = alpha * l_ref[h:h + 1, :] + jnp.sum(p, axis=0, keepdims=True)
                m_ref[h:h + 1, :] = m_new
                rows = slice(h * HEAD_DIM, (h + 1) * HEAD_DIM)
                acc_ref[rows, :] = alpha * acc_ref[rows, :] + _dot(vt, p.astype(BF16))
        return carry

    lax.fori_loop(0, nk, attn_tile, 0)

    for h in range(DSA_HEADS):
        rows = slice(h * HEAD_DIM, (h + 1) * HEAD_DIM)
        o_t = acc_ref[rows, :] * (1.0 / l_ref[h:h + 1, :])
        o_ref[:, rows] = o_t.T.astype(o_ref.dtype)


def _dsa(iqt, iwt, dqt, ik, dk, vt, batch, seq, topk):
    TQ = DSA_TQ
    nq = seq // TQ
    t = batch * seq
    nh = DSA_HEADS * HEAD_DIM
    nkv = DSA_KV_HEADS * HEAD_DIM
    col = lambda b, i: (0, b * nq + i)
    return pl.pallas_call(
        functools.partial(_dsa_body, topk),
        grid=(batch, nq),
        in_specs=[
            pl.BlockSpec((IDX_HEADS * IDX_DIM, TQ), col),
            pl.BlockSpec((IDX_HEADS, TQ), col),
            pl.BlockSpec((nh, TQ), col),
            pl.BlockSpec((seq, IDX_DIM), lambda b, i: (b, 0), pipeline_mode=pl.Buffered(1)),
            pl.BlockSpec((seq, nkv), lambda b, i: (b, 0), pipeline_mode=pl.Buffered(1)),
            pl.BlockSpec((nkv, seq), lambda b, i: (0, b), pipeline_mode=pl.Buffered(1)),
        ],
        out_specs=pl.BlockSpec((TQ, nh), lambda b, i: (b * nq + i, 0)),
        out_shape=jax.ShapeDtypeStruct((t, nh), BF16),
        scratch_shapes=[
            pltpu.VMEM((seq, TQ), jnp.int32),
            pltpu.VMEM((DSA_HEADS, TQ), F32),
            pltpu.VMEM((DSA_HEADS, TQ), F32),
            pltpu.VMEM((nh, TQ), F32),
        ],
        compiler_params=_params("parallel", "arbitrary"),
        name="dsa",
    )(iqt, iwt, dqt, ik, dk, vt)


def _merge_body(og_ref, od_ref, wg_ref, wd_ref, gg_ref, gd_ref, bgg_ref, bgd_ref, o_ref):
    yg = _dot(og_ref[...], wg_ref[...])
    yd = _dot(od_ref[...], wd_ref[...])
    sg = 1.0 / (1.0 + jnp.exp(-(gg_ref[...] + bgg_ref[...])))
    sd = 1.0 / (1.0 + jnp.exp(-(gd_ref[...] + bgd_ref[...])))
    o_ref[...] = (sg * yg + sd * yd).astype(o_ref.dtype)


def _merge(o_gla, o_dsa, wg, wd, p_nat, b_gate, gate_col, tm=512, tn=512):
    t, dg = o_gla.shape
    dd = o_dsa.shape[1]
    d = wg.shape[1]
    gb = gate_col // tn
    nb = d // tn
    return pl.pallas_call(
        _merge_body,
        grid=(t // tm, nb),
        in_specs=[
            pl.BlockSpec((tm, dg), lambda i, j: (i, 0)),
            pl.BlockSpec((tm, dd), lambda i, j: (i, 0)),
            pl.BlockSpec((dg, tn), lambda i, j: (0, j)),
            pl.BlockSpec((dd, tn), lambda i, j: (0, j)),
            pl.BlockSpec((tm, tn), lambda i, j: (i, gb + j)),
            pl.BlockSpec((tm, tn), lambda i, j: (i, gb + nb + j)),
            pl.BlockSpec((1, tn), lambda i, j: (0, j)),
            pl.BlockSpec((1, tn), lambda i, j: (0, nb + j)),
        ],
        out_specs=pl.BlockSpec((tm, tn), lambda i, j: (i, j)),
        out_shape=jax.ShapeDtypeStruct((t, d), BF16),
        compiler_params=_params("parallel", "parallel"),
        name="merge",
    )(o_gla, o_dsa, wg, wd, p_nat, p_nat, b_gate, b_gate)


def _outproj_body(x_ref, m_ref, w_ref, o_ref):
    o_ref[...] = x_ref[...] + _dot(m_ref[...], w_ref[...])


def _outproj(x2, m, w_out, tm=512, tn=512):
    t, d = x2.shape
    return pl.pallas_call(
        _outproj_body,
        grid=(t // tm, d // tn),
        in_specs=[
            pl.BlockSpec((tm, tn), lambda i, j: (i, j)),
            pl.BlockSpec((tm, d), lambda i, j: (i, 0)),
            pl.BlockSpec((d, tn), lambda i, j: (0, j)),
        ],
        out_specs=pl.BlockSpec((tm, tn), lambda i, j: (i, j)),
        out_shape=jax.ShapeDtypeStruct((t, d), F32),
        compiler_params=_params("parallel", "parallel"),
        name="out_proj",
    )(x2, m, w_out)


def _mlp_body(x_ref, g_ref, w1_ref, w2_ref, o_ref, h_scr):
    @pl.when(pl.program_id(1) == 0)
    def _():
        x = x_ref[...]
        h_scr[...] = (x * lax.rsqrt(jnp.mean(x * x, axis=-1, keepdims=True) + NORM_EPS) * g_ref[...]).astype(BF16)
        o_ref[...] = x

    u = jnp.maximum(_dot(h_scr[...], w1_ref[...]), 0.0)
    o_ref[...] += _dot((u * u).astype(BF16), w2_ref[...])


def _mlp(x1, g, w1, w2, tm=512, tf=512):
    t, d = x1.shape
    ff = w1.shape[1]
    return pl.pallas_call(
        _mlp_body,
        grid=(t // tm, ff // tf),
        in_specs=[
            pl.BlockSpec((tm, d), lambda i, f: (i, 0)),
            pl.BlockSpec((1, d), lambda i, f: (0, 0)),
            pl.BlockSpec((d, tf), lambda i, f: (0, f)),
            pl.BlockSpec((tf, d), lambda i, f: (f, 0)),
        ],
        out_specs=pl.BlockSpec((tm, d), lambda i, f: (i, 0)),
        out_shape=jax.ShapeDtypeStruct((t, d), F32),
        scratch_shapes=[pltpu.VMEM((tm, d), BF16)],
        compiler_params=_params("parallel", "arbitrary"),
        name="mlp",
    )(x1, g.reshape(1, d), w1, w2)


def _rope_tables(seq, q_g, k_g, ik_g):
    pos = jnp.arange(seq, dtype=jnp.int32).astype(F32)

    def cs(rot):
        half = rot // 2
        inv = ROPE_THETA ** (-jnp.arange(half, dtype=F32) * 2.0 / rot)
        ang = pos[:, None] * inv[None, :]
        return jnp.cos(ang), jnp.sin(ang)

    c, s = cs(HEAD_DIM)
    hh = HEAD_DIM // 2
    ka = jnp.concatenate([c, c], axis=1) * k_g[None, :]
    kb = jnp.concatenate([-s * k_g[None, hh:], s * k_g[None, :hh]], axis=1)
    qs = (HEAD_DIM ** -0.5) * LOG2E
    qa = (jnp.concatenate([c, c], axis=1) * q_g[None, :] * qs).T
    qb = (jnp.concatenate([-s * q_g[None, hh:], s * q_g[None, :hh]], axis=1) * qs).T

    ci, si = cs(IDX_ROPE_DIM)
    h2 = IDX_ROPE_DIM // 2
    one = jnp.ones((seq, IDX_DIM - IDX_ROPE_DIM), F32)
    zero = jnp.zeros((seq, IDX_DIM - IDX_ROPE_DIM), F32)
    zh = jnp.zeros((seq, h2), F32)
    ia = jnp.concatenate([ci, ci, one], axis=1) * ik_g[None, :]
    ib1 = jnp.concatenate([-si * ik_g[None, h2:2 * h2], zh, zero], axis=1)
    ib2 = jnp.concatenate([zh, si * ik_g[None, :h2], zero], axis=1)
    iqs = IDX_DIM ** -0.5
    iqa = (jnp.concatenate([ci, ci, one], axis=1) * iqs).T
    iqb = (jnp.concatenate([-si, si, zero], axis=1) * iqs).T
    return (ka, kb, ia, ib1, ib2), (qa, qb), (iqa, iqb)


def kernel(x, norm1_g, w_in, gla_wg2, gla_bg, gla_norm_g, w_proj_gla, q_norm_g, k_norm_g, idx_k_norm_g,
           w_proj_dsa, b_gate, w_out, norm2_g, w_ff1, w_ff2):
    batch, seq, d = x.shape
    depth = w_in.shape[0]
    topk = min(TOPK_MAX, seq // 4)
    nqk = GLA_HEADS * GLA_DK
    nv = GLA_HEADS * GLA_DV
    ndq = DSA_HEADS * HEAD_DIM
    nkv = DSA_KV_HEADS * HEAD_DIM
    niq = IDX_HEADS * IDX_DIM
    splits = (nqk, nqk, nv, nv, GLA_GATE_RANK, ndq, nkv, nkv, niq, IDX_DIM, IDX_HEADS, 2 * d)
    offs = [0]
    for n in splits:
        offs.append(offs[-1] + n)
    o_gq, _, _, _, o_glr, o_dq, o_dk, o_dv, o_iq, o_ik, o_iw, o_gate, _ = offs
    n_gla_cols = 2 * nqk + 2 * nv
    assert n_gla_cols % 512 == 0 and seq % 512 == 0 and d % 512 == 0

    x2 = x.reshape(batch * seq, d)
    for l in range(depth):
        w = w_in[l]
        w_nat = jnp.concatenate([w[:, o_gq:o_glr], w[:, o_gate:]], axis=1).astype(BF16)
        glr_pad = jnp.zeros((d, LANES - GLA_GATE_RANK), F32)
        w_k = jnp.concatenate([w[:, o_dk:o_dv], w[:, o_ik:o_iw], w[:, o_glr:o_dq], glr_pad], axis=1).astype(BF16)
        wt_dq = w[:, o_dq:o_dk].T.astype(BF16)
        wt_iq = w[:, o_iq:o_ik].T.astype(BF16)
        wt_v = w[:, o_dv:o_iq].T.astype(BF16)
        wt_iw = w[:, o_iw:o_gate].T.astype(BF16)
        wg2p = jnp.concatenate(
            [gla_wg2[l], jnp.zeros((LANES - GLA_GATE_RANK, nqk), F32)], axis=0).astype(BF16)
        k_tabs, q_tabs, iq_tabs = _rope_tables(seq, q_norm_g[l], k_norm_g[l], idx_k_norm_g[l])

        h, ht = _rmsnorm_both(x2, norm1_g[l])
        p_nat = _mm(h, w_nat, F32, 1024, 512, "proj_nat")
        dk, ik, glr = _proj_k(h, w_k, k_tabs, seq)
        dqt = _proj_t("dq", wt_dq, ht, q_tabs, seq, BF16, 1024)
        iqt = _proj_t("iq", wt_iq, ht, iq_tabs, seq, BF16, 1024)
        vt = _proj_t("v", wt_v, ht, (), seq, BF16, nkv)
        iwt = _proj_t("iw", wt_iw, ht, (), seq, F32, IDX_HEADS)

        o_gla = _gla(p_nat, glr, wg2p, gla_bg[l].reshape(1, nqk), gla_norm_g[l].reshape(1, GLA_DV), batch, seq)
        o_dsa = _dsa(iqt, iwt, dqt, ik, dk, vt, batch, seq, topk)

        m = _merge(o_gla, o_dsa, w_proj_gla[l].astype(BF16), w_proj_dsa[l].astype(BF16), p_nat,
                   b_gate[l].reshape(1, 2 * d), n_gla_cols)
        x1 = _outproj(x2, m, w_out[l].astype(BF16))
        x2 = _mlp(x1, norm2_g[l], w_ff1[l].astype(BF16), w_ff2[l].astype(BF16))
    return x2.reshape(batch, seq, d)
```

```python
import functools
import math

import jax
import jax.numpy as jnp
from jax import lax
from jax.experimental import pallas as pl
from jax.experimental.pallas import tpu as pltpu

BF16 = jnp.bfloat16
F32 = jnp.float32

NORM_EPS = 1e-6
ROPE_THETA = 10000.0
GLA_HEADS = 4
GLA_DK = 256
GLA_DV = 512
GLA_GATE_RANK = 16
GLA_GATE_NORMALIZER = 16.0
DSA_HEADS = 16
DSA_KV_HEADS = 4
HEAD_DIM = 128
IDX_HEADS = 16
IDX_DIM = 128
IDX_ROPE_DIM = 64
TOPK_MAX = 256

LANES = 128
VMEM_LIMIT = 56 * 1024 * 1024
GLA_CHUNK = 128
GLA_SUB = 16
DSA_TQ = 256
DSA_TK = 512
DSA_ATK = 512
DSA_COUNT_ROWS = 64
DSA_BITS_PER_TRIP = 2
DSA_MIN_DENOM = 2.0 ** -80
INT_MIN = -(2 ** 31)
NEG_BIG = -1e30
LOG2E = math.log2(math.e)


def _params(*sem):
    return pltpu.CompilerParams(dimension_semantics=sem, vmem_limit_bytes=VMEM_LIMIT)


def _dot(a, b):
    return jnp.dot(a, b, preferred_element_type=F32)


def _rms_body(x_ref, g_ref, h_ref, ht_ref):
    x = x_ref[...]
    y = x * lax.rsqrt(jnp.mean(x * x, axis=-1, keepdims=True) + NORM_EPS) * g_ref[...]
    h_ref[...] = y.astype(BF16)
    ht_ref[...] = y.T.astype(BF16)


def _rmsnorm_both(x2, g, tm=256):
    t, d = x2.shape
    return pl.pallas_call(
        _rms_body,
        grid=(t // tm,),
        in_specs=[pl.BlockSpec((tm, d), lambda i: (i, 0)), pl.BlockSpec((1, d), lambda i: (0, 0))],
        out_specs=[pl.BlockSpec((tm, d), lambda i: (i, 0)), pl.BlockSpec((d, tm), lambda i: (0, i))],
        out_shape=[jax.ShapeDtypeStruct((t, d), BF16), jax.ShapeDtypeStruct((d, t), BF16)],
        compiler_params=_params("parallel"),
        name="rmsnorm_in",
    )(x2, g.reshape(1, d))


def _mm_body(a_ref, w_ref, o_ref):
    o_ref[...] = _dot(a_ref[...], w_ref[...]).astype(o_ref.dtype)


def _mm(a, w, out_dtype, tm, tn, name):
    m, k = a.shape
    n = w.shape[1]
    return pl.pallas_call(
        _mm_body,
        grid=(m // tm, n // tn),
        in_specs=[pl.BlockSpec((tm, k), lambda i, j: (i, 0)), pl.BlockSpec((k, tn), lambda i, j: (0, j))],
        out_specs=pl.BlockSpec((tm, tn), lambda i, j: (i, j)),
        out_shape=jax.ShapeDtypeStruct((m, n), out_dtype),
        compiler_params=_params("parallel", "parallel"),
        name=name,
    )(a, w)


def _projk_body(h_ref, w_ref, ka_ref, kb_ref, ia_ref, ib1_ref, ib2_ref, dk_ref, ik_ref, glr_ref):
    p = _dot(h_ref[...], w_ref[...])
    nkv = DSA_KV_HEADS
    for j in range(nkv):
        pj = p[:, j * HEAD_DIM:(j + 1) * HEAD_DIM]
        n = pj * lax.rsqrt(jnp.mean(pj * pj, axis=-1, keepdims=True) + NORM_EPS)
        out = n * ka_ref[...] + pltpu.roll(n, HEAD_DIM // 2, 1) * kb_ref[...]
        dk_ref[:, j * HEAD_DIM:(j + 1) * HEAD_DIM] = out.astype(BF16)
    c0 = nkv * HEAD_DIM
    pi = p[:, c0:c0 + IDX_DIM]
    n = pi * lax.rsqrt(jnp.mean(pi * pi, axis=-1, keepdims=True) + NORM_EPS)
    half = IDX_ROPE_DIM // 2
    out = n * ia_ref[...] + pltpu.roll(n, IDX_DIM - half, 1) * ib1_ref[...] + pltpu.roll(n, half, 1) * ib2_ref[...]
    ik_ref[...] = out.astype(BF16)
    glr_ref[...] = p[:, c0 + IDX_DIM:]


def _proj_k(h, w_k, tabs, seq, tm=512):
    t, d = h.shape
    nblk = seq // tm
    ncol = w_k.shape[1]
    tab_spec = pl.BlockSpec((tm, LANES), lambda i: (i % nblk, 0))
    return pl.pallas_call(
        _projk_body,
        grid=(t // tm,),
        in_specs=[pl.BlockSpec((tm, d), lambda i: (i, 0)), pl.BlockSpec((d, ncol), lambda i: (0, 0))] + [tab_spec] * 5,
        out_specs=[pl.BlockSpec((tm, DSA_KV_HEADS * HEAD_DIM), lambda i: (i, 0)),
                   pl.BlockSpec((tm, IDX_DIM), lambda i: (i, 0)),
                   pl.BlockSpec((tm, LANES), lambda i: (i, 0))],
        out_shape=[jax.ShapeDtypeStruct((t, DSA_KV_HEADS * HEAD_DIM), BF16),
                   jax.ShapeDtypeStruct((t, IDX_DIM), BF16),
                   jax.ShapeDtypeStruct((t, LANES), F32)],
        compiler_params=_params("parallel"),
        name="proj_k",
    )(h, w_k, *tabs)


def _projt_body(kind, w_ref, ht_ref, *rest):
    p = _dot(w_ref[...], ht_ref[...])
    o_ref = rest[-1]
    rows = p.shape[0]
    if kind == "v":
        o_ref[...] = p.astype(o_ref.dtype)
        return
    if kind == "iw":
        o_ref[...] = p * (IDX_HEADS ** -0.5)
        return
    ta_ref, tb_ref = rest[0], rest[1]
    for j in range(rows // HEAD_DIM):
        pj = p[j * HEAD_DIM:(j + 1) * HEAD_DIM]
        if kind == "dq":
            n = pj * lax.rsqrt(jnp.mean(pj * pj, axis=0, keepdims=True) + NORM_EPS)
            sw = jnp.concatenate([n[HEAD_DIM // 2:], n[:HEAD_DIM // 2]], axis=0)
        else:
            n = pj
            half = IDX_ROPE_DIM // 2
            sw = jnp.concatenate([n[half:2 * half], n[:half], n[2 * half:]], axis=0)
        o_ref[j * HEAD_DIM:(j + 1) * HEAD_DIM, :] = (n * ta_ref[...] + sw * tb_ref[...]).astype(o_ref.dtype)


def _proj_t(kind, w_t, ht, tabs, seq, out_dtype, tc, tt=512):
    c, d = w_t.shape
    t = ht.shape[1]
    nblk = seq // tt
    tab_spec = pl.BlockSpec((HEAD_DIM, tt), lambda i, j: (0, j % nblk))
    return pl.pallas_call(
        functools.partial(_projt_body, kind),
        grid=(c // tc, t // tt),
        in_specs=[pl.BlockSpec((tc, d), lambda i, j: (i, 0)), pl.BlockSpec((d, tt), lambda i, j: (0, j))]
        + [tab_spec] * len(tabs),
        out_specs=pl.BlockSpec((tc, tt), lambda i, j: (i, j)),
        out_shape=jax.ShapeDtypeStruct((c, t), out_dtype),
        compiler_params=_params("parallel", "parallel"),
        name="proj_t_" + kind,
    )(w_t, ht, *tabs)


def _gla_body(q_ref, k_ref, v_ref, gr_ref, glr_ref, wg2_ref, bg_ref, ng_ref, o_ref, st_ref, b_scr):
    C, SB = GLA_CHUNK, GLA_SUB

    @pl.when(pl.program_id(2) == 0)
    def _():
        st_ref[...] = jnp.zeros_like(st_ref)

    q = q_ref[...] * (GLA_DK ** -0.5)
    k = k_ref[...]
    vb = v_ref[...].astype(BF16)
    z = _dot(glr_ref[...].astype(BF16), wg2_ref[...]) + bg_ref[...]
    g = (jnp.minimum(z, 0.0) - jnp.log1p(jnp.exp(-jnp.abs(z)))) * (1.0 / GLA_GATE_NORMALIZER)

    ri = lax.broadcasted_iota(jnp.int32, (C, C), 0)
    ci = lax.broadcasted_iota(jnp.int32, (C, C), 1)
    tri = jnp.where(ci <= ri, 1.0, 0.0).astype(BF16)
    g1 = g.astype(BF16)
    r1 = g - g1.astype(F32)
    g2 = r1.astype(BF16)
    g3 = (r1 - g2.astype(F32)).astype(BF16)
    b = _dot(tri, g1) + _dot(tri, g2) + _dot(tri, g3)
    b_scr[...] = b
    kt = k.T
    bt = b.T

    o = _dot((q * jnp.exp(b)).astype(BF16), st_ref[...].astype(BF16))

    lane = lax.broadcasted_iota(jnp.int32, (SB, C), 1)
    rowi = lax.broadcasted_iota(jnp.int32, (SB, 1), 0)
    blocks = []
    for blk in range(C // SB):
        s = blk * SB
        q_i = q[s:s + SB]
        b_i = b[s:s + SB]
        a = jnp.zeros((SB, C), F32)
        for j in range(SB):
            r = s + j
            dec = jnp.where(rowi >= j, jnp.exp(b_i - b_scr[r:r + 1, :]), 0.0)
            col = jnp.sum(q_i * k_ref[r:r + 1, :] * dec, axis=-1, keepdims=True)
            a = jnp.where(lane == r, col, a)
        if blk > 0:
            q_s = q_i * jnp.exp(b_i - b_scr[s - 1:s, :])
            k_s = kt * jnp.exp(jnp.minimum(bt[:, s - 1:s] - bt, 0.0))
            a = jnp.where(lane < s, _dot(q_s.astype(BF16), k_s.astype(BF16)), a)
        blocks.append(a)
    a_full = jnp.concatenate(blocks, axis=0)
    o = o + _dot(a_full.astype(BF16), vb)

    b_last = bt[:, C - 1:C]
    k_dec = kt * jnp.exp(b_last - bt)
    st_ref[...] = st_ref[...] * jnp.exp(b_last) + _dot(k_dec.astype(BF16), vb)

    on = o * lax.rsqrt(jnp.mean(o * o, axis=-1, keepdims=True) + NORM_EPS) * ng_ref[...]
    gr = gr_ref[...]
    o_ref[...] = (on * gr * (1.0 / (1.0 + jnp.exp(-gr)))).astype(o_ref.dtype)


def _gla(p_nat, glr, wg2p, bg, ng, batch, seq):
    C = GLA_CHUNK
    nc = seq // C
    t = batch * seq
    kq = GLA_HEADS * GLA_DK // GLA_DK
    kv = 2 * GLA_HEADS * GLA_DK // GLA_DV
    row = lambda b, h, c: b * nc + c
    return pl.pallas_call(
        _gla_body,
        grid=(batch, GLA_HEADS, nc),
        in_specs=[
            pl.BlockSpec((C, GLA_DK), lambda b, h, c: (row(b, h, c), h)),
            pl.BlockSpec((C, GLA_DK), lambda b, h, c: (row(b, h, c), kq + h)),
            pl.BlockSpec((C, GLA_DV), lambda b, h, c: (row(b, h, c), kv + h)),
            pl.BlockSpec((C, GLA_DV), lambda b, h, c: (row(b, h, c), kv + GLA_HEADS + h)),
            pl.BlockSpec((C, LANES), lambda b, h, c: (row(b, h, c), 0)),
            pl.BlockSpec((LANES, GLA_DK), lambda b, h, c: (0, h)),
            pl.BlockSpec((1, GLA_DK), lambda b, h, c: (0, h)),
            pl.BlockSpec((1, GLA_DV), lambda b, h, c: (0, 0)),
        ],
        out_specs=pl.BlockSpec((C, GLA_DV), lambda b, h, c: (row(b, h, c), h)),
        out_shape=jax.ShapeDtypeStruct((t, GLA_HEADS * GLA_DV), BF16),
        scratch_shapes=[pltpu.VMEM((GLA_DK, GLA_DV), F32), pltpu.VMEM((C, GLA_DK), F32)],
        compiler_params=_params("parallel", "parallel", "arbitrary"),
        name="gla",
    )(p_nat, p_nat, p_nat, p_nat, glr, wg2p, bg, ng)


def _dsa_body(topk, bound_ref, iqt_ref, iwt_ref, dqt_ref, ik_ref, dk_ref, vt_ref, o_ref,
              sc_ref, qg_ref, m_ref, l_ref, acc_ref):
    TQ, TK = DSA_TQ, DSA_TK
    i = pl.program_id(1)
    nk = (i * TQ + TQ + TK - 1) // TK
    q_pos = i * TQ + lax.broadcasted_iota(jnp.int32, (TK, TQ), 1)
    k_off = lax.broadcasted_iota(jnp.int32, (TK, TQ), 0)

    def score_tile(j, carry):
        base = pl.multiple_of(j * TK, TK)
        ikt = ik_ref[pl.ds(base, TK), :]
        acc = jnp.zeros((TK, TQ), F32)
        for h in range(IDX_HEADS):
            r = _dot(ikt, iqt_ref[h * IDX_DIM:(h + 1) * IDX_DIM, :])
            acc = acc + jnp.maximum(r, 0.0) * iwt_ref[h:h + 1, :]
        bits = pltpu.bitcast(acc, jnp.int32)
        key = jnp.where(bits < 0, bits ^ jnp.int32(0x7FFFFFFF), bits)
        key = jnp.where(base + k_off <= q_pos, key, jnp.int32(INT_MIN))
        sc_ref[pl.ds(base, TK), :] = key
        return carry

    lax.fori_loop(0, nk, score_tile, 0)

    CR = DSA_COUNT_ROWS

    def count_ge(cand):
        def body(j, cnt):
            base = pl.multiple_of(j * TK, TK)
            hit = jnp.where(sc_ref[pl.ds(base, TK), :] >= cand, 1.0, 0.0)
            return cnt + jnp.sum(hit.reshape(TK // CR, CR, TQ), axis=0)
        cnt = lax.fori_loop(0, nk, body, jnp.zeros((CR, TQ), F32))
        return jnp.sum(cnt, axis=0, keepdims=True)

    n_adm = i * TQ + 1 + lax.broadcasted_iota(jnp.int32, (1, TQ), 1)
    zero = jnp.zeros((1, TQ), jnp.int32)
    c0 = count_ge(zero)
    pos = c0 >= topk
    thr0 = jnp.where(pos, zero, jnp.int32(INT_MIN))
    cnt0 = jnp.where(pos, c0, 0.0)

    def open_rows(thr_cnt):
        return jnp.max(jnp.where((thr_cnt != topk) & (n_adm > topk), 1, 0))

    def search_cond(state):
        bit, _, _, pending = state
        return (bit >= 0) & (pending > 0)

    def search_step(state):
        bit, thr, thr_cnt, _ = state
        for sub in range(DSA_BITS_PER_TRIP):
            live = bit - sub >= 0
            cand = thr | jnp.left_shift(jnp.int32(1), jnp.maximum(bit - sub, 0))
            c = count_ge(cand)
            take = (c >= topk) & live
            thr = jnp.where(take, cand, thr)
            thr_cnt = jnp.where(take, c, thr_cnt)
        return bit - DSA_BITS_PER_TRIP, thr, thr_cnt, open_rows(thr_cnt)

    _, thr, _, _ = lax.while_loop(search_cond, search_step, (jnp.int32(30), thr0, cnt0, open_rows(cnt0)))
    thr = jnp.maximum(thr, jnp.int32(INT_MIN + 1))

    grp = DSA_HEADS // DSA_KV_HEADS
    AK = DSA_ATK
    na = (i * TQ + TQ + AK - 1) // AK
    for g in range(DSA_KV_HEADS):
        qg_ref[g] = jnp.concatenate(
            [dqt_ref[(g * grp + hh) * HEAD_DIM:(g * grp + hh + 1) * HEAD_DIM, :] for hh in range(grp)], axis=1)

    def tile_operands(j, g):
        base = pl.multiple_of(j * AK, AK)
        kt = dk_ref[pl.ds(base, AK), g * HEAD_DIM:(g + 1) * HEAD_DIM]
        vt = vt_ref[g * HEAD_DIM:(g + 1) * HEAD_DIM, pl.ds(base, AK)]
        return kt, vt

    def tile_bias(j, selected):
        base = pl.multiple_of(j * AK, AK)
        bias = jnp.where(sc_ref[pl.ds(base, AK), :] >= thr, selected, NEG_BIG)
        return jnp.concatenate([bias] * grp, axis=1)

    l_ref[...] = jnp.zeros_like(l_ref)
    acc_ref[...] = jnp.zeros_like(acc_ref)
    neg_bound = -bound_ref[0, 0]

    def bounded_tile(j, carry):
        bias = tile_bias(j, neg_bound)
        for g in range(DSA_KV_HEADS):
            kt, vt = tile_operands(j, g)
            p = jnp.exp2(_dot(kt, qg_ref[g]) + bias)
            l_ref[g] += jnp.sum(p, axis=0, keepdims=True)
            acc_ref[g] += _dot(vt, p.astype(BF16))
        return carry

    lax.fori_loop(0, na, bounded_tile, 0)

    @pl.when(jnp.logical_not(jnp.min(l_ref[...]) >= DSA_MIN_DENOM))
    def _():
        m_ref[...] = jnp.full_like(m_ref, NEG_BIG)
        l_ref[...] = jnp.zeros_like(l_ref)
        acc_ref[...] = jnp.zeros_like(acc_ref)

        def online_tile(j, carry):
            bias = tile_bias(j, 0.0)
            for g in range(DSA_KV_HEADS):
                kt, vt = tile_operands(j, g)
                s = _dot(kt, qg_ref[g]) + bias
                m_old = m_ref[g]
                m_new = jnp.maximum(m_old, jnp.max(s, axis=0, keepdims=True))
                p = jnp.exp2(s - m_new)
                alpha = jnp.exp2(m_old - m_new)
                l_ref[g] = alpha * l_ref[g] + jnp.sum(p, axis=0, keepdims=True)
                m_ref[g] = m_new
                acc_ref[g] = alpha * acc_ref[g] + _dot(vt, p.astype(BF16))
            return carry

        lax.fori_loop(0, na, online_tile, 0)

    for g in range(DSA_KV_HEADS):
        o_g = acc_ref[g] * (1.0 / l_ref[g])
        for hh in range(grp):
            h = g * grp + hh
            o_ref[:, h * HEAD_DIM:(h + 1) * HEAD_DIM] = o_g[:, hh * TQ:(hh + 1) * TQ].T.astype(o_ref.dtype)


def _dsa(bound, iqt, iwt, dqt, ik, dk, vt, batch, seq, topk):
    TQ = DSA_TQ
    nq = seq // TQ
    t = batch * seq
    nh = DSA_HEADS * HEAD_DIM
    nkv = DSA_KV_HEADS * HEAD_DIM
    grp = DSA_HEADS // DSA_KV_HEADS
    col = lambda b, i: (0, b * nq + i)
    return pl.pallas_call(
        functools.partial(_dsa_body, topk),
        grid=(batch, nq),
        in_specs=[
            pl.BlockSpec(memory_space=pltpu.SMEM),
            pl.BlockSpec((IDX_HEADS * IDX_DIM, TQ), col),
            pl.BlockSpec((IDX_HEADS, TQ), col),
            pl.BlockSpec((nh, TQ), col),
            pl.BlockSpec((seq, IDX_DIM), lambda b, i: (b, 0), pipeline_mode=pl.Buffered(1)),
            pl.BlockSpec((seq, nkv), lambda b, i: (b, 0), pipeline_mode=pl.Buffered(1)),
            pl.BlockSpec((nkv, seq), lambda b, i: (0, b), pipeline_mode=pl.Buffered(1)),
        ],
        out_specs=pl.BlockSpec((TQ, nh), lambda b, i: (b * nq + i, 0)),
        out_shape=jax.ShapeDtypeStruct((t, nh), BF16),
        scratch_shapes=[
            pltpu.VMEM((seq, TQ), jnp.int32),
            pltpu.VMEM((DSA_KV_HEADS, HEAD_DIM, grp * TQ), BF16),
            pltpu.VMEM((DSA_KV_HEADS, 1, grp * TQ), F32),
            pltpu.VMEM((DSA_KV_HEADS, 1, grp * TQ), F32),
            pltpu.VMEM((DSA_KV_HEADS, HEAD_DIM, grp * TQ), F32),
        ],
        compiler_params=_params("parallel", "arbitrary"),
        name="dsa",
    )(bound, iqt, iwt, dqt, ik, dk, vt)


def _merge_body(og_ref, od_ref, wg_ref, wd_ref, gg_ref, gd_ref, bgg_ref, bgd_ref, o_ref):
    yg = _dot(og_ref[...], wg_ref[...])
    yd = _dot(od_ref[...], wd_ref[...])
    sg = 1.0 / (1.0 + jnp.exp(-(gg_ref[...] + bgg_ref[...])))
    sd = 1.0 / (1.0 + jnp.exp(-(gd_ref[...] + bgd_ref[...])))
    o_ref[...] = (sg * yg + sd * yd).astype(o_ref.dtype)


def _merge(o_gla, o_dsa, wg, wd, p_nat, b_gate, gate_col, tm=512, tn=512):
    t, dg = o_gla.shape
    dd = o_dsa.shape[1]
    d = wg.shape[1]
    gb = gate_col // tn
    nb = d // tn
    return pl.pallas_call(
        _merge_body,
        grid=(t // tm, nb),
        in_specs=[
            pl.BlockSpec((tm, dg), lambda i, j: (i, 0)),
            pl.BlockSpec((tm, dd), lambda i, j: (i, 0)),
            pl.BlockSpec((dg, tn), lambda i, j: (0, j)),
            pl.BlockSpec((dd, tn), lambda i, j: (0, j)),
            pl.BlockSpec((tm, tn), lambda i, j: (i, gb + j)),
            pl.BlockSpec((tm, tn), lambda i, j: (i, gb + nb + j)),
            pl.BlockSpec((1, tn), lambda i, j: (0, j)),
            pl.BlockSpec((1, tn), lambda i, j: (0, nb + j)),
        ],
        out_specs=pl.BlockSpec((tm, tn), lambda i, j: (i, j)),
        out_shape=jax.ShapeDtypeStruct((t, d), BF16),
        compiler_params=_params("parallel", "parallel"),
        name="merge",
    )(o_gla, o_dsa, wg, wd, p_nat, p_nat, b_gate, b_gate)


def _outproj_body(x_ref, m_ref, w_ref, o_ref):
    o_ref[...] = x_ref[...] + _dot(m_ref[...], w_ref[...])


def _outproj(x2, m, w_out, tm=512, tn=512):
    t, d = x2.shape
    return pl.pallas_call(
        _outproj_body,
        grid=(t // tm, d // tn),
        in_specs=[
            pl.BlockSpec((tm, tn), lambda i, j: (i, j)),
            pl.BlockSpec((tm, d), lambda i, j: (i, 0)),
            pl.BlockSpec((d, tn), lambda i, j: (0, j)),
        ],
        out_specs=pl.BlockSpec((tm, tn), lambda i, j: (i, j)),
        out_shape=jax.ShapeDtypeStruct((t, d), F32),
        compiler_params=_params("parallel", "parallel"),
        name="out_proj",
    )(x2, m, w_out)


def _mlp_body(x_ref, g_ref, w1_ref, w2_ref, o_ref, h_scr):
    @pl.when(pl.program_id(1) == 0)
    def _():
        x = x_ref[...]
        h_scr[...] = (x * lax.rsqrt(jnp.mean(x * x, axis=-1, keepdims=True) + NORM_EPS) * g_ref[...]).astype(BF16)
        o_ref[...] = x

    u = jnp.maximum(_dot(h_scr[...], w1_ref[...]), 0.0)
    o_ref[...] += _dot((u * u).astype(BF16), w2_ref[...])


def _mlp(x1, g, w1, w2, tm=512, tf=512):
    t, d = x1.shape
    ff = w1.shape[1]
    return pl.pallas_call(
        _mlp_body,
        grid=(t // tm, ff // tf),
        in_specs=[
            pl.BlockSpec((tm, d), lambda i, f: (i, 0)),
            pl.BlockSpec((1, d), lambda i, f: (0, 0)),
            pl.BlockSpec((d, tf), lambda i, f: (0, f)),
            pl.BlockSpec((tf, d), lambda i, f: (f, 0)),
        ],
        out_specs=pl.BlockSpec((tm, d), lambda i, f: (i, 0)),
        out_shape=jax.ShapeDtypeStruct((t, d), F32),
        scratch_shapes=[pltpu.VMEM((tm, d), BF16)],
        compiler_params=_params("parallel", "arbitrary"),
        name="mlp",
    )(x1, g.reshape(1, d), w1, w2)


def _rope_tables(seq, q_g, k_g, ik_g):
    pos = jnp.arange(seq, dtype=jnp.int32).astype(F32)

    def cs(rot):
        half = rot // 2
        inv = ROPE_THETA ** (-jnp.arange(half, dtype=F32) * 2.0 / rot)
        ang = pos[:, None] * inv[None, :]
        return jnp.cos(ang), jnp.sin(ang)

    c, s = cs(HEAD_DIM)
    hh = HEAD_DIM // 2
    ka = jnp.concatenate([c, c], axis=1) * k_g[None, :]
    kb = jnp.concatenate([-s * k_g[None, hh:], s * k_g[None, :hh]], axis=1)
    qs = (HEAD_DIM ** -0.5) * LOG2E
    qa = (jnp.concatenate([c, c], axis=1) * q_g[None, :] * qs).T
    qb = (jnp.concatenate([-s * q_g[None, hh:], s * q_g[None, :hh]], axis=1) * qs).T

    ci, si = cs(IDX_ROPE_DIM)
    h2 = IDX_ROPE_DIM // 2
    one = jnp.ones((seq, IDX_DIM - IDX_ROPE_DIM), F32)
    zero = jnp.zeros((seq, IDX_DIM - IDX_ROPE_DIM), F32)
    zh = jnp.zeros((seq, h2), F32)
    ia = jnp.concatenate([ci, ci, one], axis=1) * ik_g[None, :]
    ib1 = jnp.concatenate([-si * ik_g[None, h2:2 * h2], zh, zero], axis=1)
    ib2 = jnp.concatenate([zh, si * ik_g[None, :h2], zero], axis=1)
    iqs = IDX_DIM ** -0.5
    iqa = (jnp.concatenate([ci, ci, one], axis=1) * iqs).T
    iqb = (jnp.concatenate([-si, si, zero], axis=1) * iqs).T
    return (ka, kb, ia, ib1, ib2), (qa, qb), (iqa, iqb)


def kernel(x, norm1_g, w_in, gla_wg2, gla_bg, gla_norm_g, w_proj_gla, q_norm_g, k_norm_g, idx_k_norm_g,
           w_proj_dsa, b_gate, w_out, norm2_g, w_ff1, w_ff2):
    batch, seq, d = x.shape
    depth = w_in.shape[0]
    topk = min(TOPK_MAX, seq // 4)
    nqk = GLA_HEADS * GLA_DK
    nv = GLA_HEADS * GLA_DV
    ndq = DSA_HEADS * HEAD_DIM
    nkv = DSA_KV_HEADS * HEAD_DIM
    niq = IDX_HEADS * IDX_DIM
    splits = (nqk, nqk, nv, nv, GLA_GATE_RANK, ndq, nkv, nkv, niq, IDX_DIM, IDX_HEADS, 2 * d)
    offs = [0]
    for n in splits:
        offs.append(offs[-1] + n)
    o_gq, _, _, _, o_glr, o_dq, o_dk, o_dv, o_iq, o_ik, o_iw, o_gate, _ = offs
    n_gla_cols = 2 * nqk + 2 * nv
    assert n_gla_cols % 512 == 0 and seq % 512 == 0 and d % 512 == 0

    x2 = x.reshape(batch * seq, d)
    for l in range(depth):
        w = w_in[l]
        w_nat = jnp.concatenate([w[:, o_gq:o_glr], w[:, o_gate:]], axis=1).astype(BF16)
        glr_pad = jnp.zeros((d, LANES - GLA_GATE_RANK), F32)
        w_k = jnp.concatenate([w[:, o_dk:o_dv], w[:, o_ik:o_iw], w[:, o_glr:o_dq], glr_pad], axis=1).astype(BF16)
        wt_dq = w[:, o_dq:o_dk].T.astype(BF16)
        wt_iq = w[:, o_iq:o_ik].T.astype(BF16)
        wt_v = w[:, o_dv:o_iq].T.astype(BF16)
        wt_iw = w[:, o_iw:o_gate].T.astype(BF16)
        wg2p = jnp.concatenate(
            [gla_wg2[l], jnp.zeros((LANES - GLA_GATE_RANK, nqk), F32)], axis=0).astype(BF16)
        k_tabs, q_tabs, iq_tabs = _rope_tables(seq, q_norm_g[l], k_norm_g[l], idx_k_norm_g[l])

        h, ht = _rmsnorm_both(x2, norm1_g[l])
        p_nat = _mm(h, w_nat, F32, 1024, 512, "proj_nat")
        dk, ik, glr = _proj_k(h, w_k, k_tabs, seq)
        dqt = _proj_t("dq", wt_dq, ht, q_tabs, seq, BF16, 1024)
        iqt = _proj_t("iq", wt_iq, ht, iq_tabs, seq, BF16, 1024)
        vt = _proj_t("v", wt_v, ht, (), seq, BF16, nkv)
        iwt = _proj_t("iw", wt_iw, ht, (), seq, F32, IDX_HEADS)

        o_gla = _gla(p_nat, glr, wg2p, gla_bg[l].reshape(1, nqk), gla_norm_g[l].reshape(1, GLA_DV), batch, seq)
        bound = (1.02 * HEAD_DIM * (HEAD_DIM ** -0.5) * LOG2E
                 * jnp.max(jnp.abs(q_norm_g[l])) * jnp.max(jnp.abs(k_norm_g[l]))).reshape(1, 1).astype(F32)
        o_dsa = _dsa(bound, iqt, iwt, dqt, ik, dk, vt, batch, seq, topk)

        m = _merge(o_gla, o_dsa, w_proj_gla[l].astype(BF16), w_proj_dsa[l].astype(BF16), p_nat,
                   b_gate[l].reshape(1, 2 * d), n_gla_cols)
        x1 = _outproj(x2, m, w_out[l].astype(BF16))
        x2 = _mlp(x1, norm2_g[l], w_ff1[l].astype(BF16), w_ff2[l].astype(BF16))
    return x2.reshape(batch, seq, d)
```

```python
import functools
import math

import jax
import jax.numpy as jnp
from jax import lax
from jax.experimental import pallas as pl
from jax.experimental.pallas import tpu as pltpu

BF16 = jnp.bfloat16
F32 = jnp.float32

NORM_EPS = 1e-6
ROPE_THETA = 10000.0
GLA_HEADS = 4
GLA_DK = 256
GLA_DV = 512
GLA_GATE_RANK = 16
GLA_GATE_NORMALIZER = 16.0
DSA_HEADS = 16
DSA_KV_HEADS = 4
HEAD_DIM = 128
IDX_HEADS = 16
IDX_DIM = 128
IDX_ROPE_DIM = 64
TOPK_MAX = 256

LANES = 128
VMEM_LIMIT = 56 * 1024 * 1024
GLA_CHUNK = 128
GLA_SUB = 16
GLA_MAX_SPLIT_LOG2 = 100.0
DSA_TQ = 256
DSA_TK = 512
DSA_ATK = 512
DSA_COUNT_ROWS = 64
DSA_MIN_DENOM = 2.0 ** -80
INT_MIN = -(2 ** 31)
HALF_BIAS = 2 ** 15
NEG_BIG = -1e30
LOG2E = math.log2(math.e)


def _params(*sem):
    return pltpu.CompilerParams(dimension_semantics=sem, vmem_limit_bytes=VMEM_LIMIT)


def _dot(a, b):
    return jnp.dot(a, b, preferred_element_type=F32)


def _rms_body(x_ref, g_ref, h_ref, ht_ref):
    x = x_ref[...]
    y = x * lax.rsqrt(jnp.mean(x * x, axis=-1, keepdims=True) + NORM_EPS) * g_ref[...]
    h_ref[...] = y.astype(BF16)
    ht_ref[...] = y.T.astype(BF16)


def _rmsnorm_both(x2, g, tm=256):
    t, d = x2.shape
    return pl.pallas_call(
        _rms_body,
        grid=(t // tm,),
        in_specs=[pl.BlockSpec((tm, d), lambda i: (i, 0)), pl.BlockSpec((1, d), lambda i: (0, 0))],
        out_specs=[pl.BlockSpec((tm, d), lambda i: (i, 0)), pl.BlockSpec((d, tm), lambda i: (0, i))],
        out_shape=[jax.ShapeDtypeStruct((t, d), BF16), jax.ShapeDtypeStruct((d, t), BF16)],
        compiler_params=_params("parallel"),
        name="rmsnorm_in",
    )(x2, g.reshape(1, d))


def _mm_body(a_ref, w_ref, o_ref):
    o_ref[...] = _dot(a_ref[...], w_ref[...]).astype(o_ref.dtype)


def _mm(a, w, out_dtype, tm, tn, name):
    m, k = a.shape
    n = w.shape[1]
    return pl.pallas_call(
        _mm_body,
        grid=(m // tm, n // tn),
        in_specs=[pl.BlockSpec((tm, k), lambda i, j: (i, 0)), pl.BlockSpec((k, tn), lambda i, j: (0, j))],
        out_specs=pl.BlockSpec((tm, tn), lambda i, j: (i, j)),
        out_shape=jax.ShapeDtypeStruct((m, n), out_dtype),
        compiler_params=_params("parallel", "parallel"),
        name=name,
    )(a, w)


def _projk_body(h_ref, w_ref, ka_ref, kb_ref, ia_ref, ib1_ref, ib2_ref, dk_ref, ik_ref, glr_ref):
    p = _dot(h_ref[...], w_ref[...])
    nkv = DSA_KV_HEADS
    for j in range(nkv):
        pj = p[:, j * HEAD_DIM:(j + 1) * HEAD_DIM]
        n = pj * lax.rsqrt(jnp.mean(pj * pj, axis=-1, keepdims=True) + NORM_EPS)
        out = n * ka_ref[...] + pltpu.roll(n, HEAD_DIM // 2, 1) * kb_ref[...]
        dk_ref[:, j * HEAD_DIM:(j + 1) * HEAD_DIM] = out.astype(BF16)
    c0 = nkv * HEAD_DIM
    pi = p[:, c0:c0 + IDX_DIM]
    n = pi * lax.rsqrt(jnp.mean(pi * pi, axis=-1, keepdims=True) + NORM_EPS)
    half = IDX_ROPE_DIM // 2
    out = n * ia_ref[...] + pltpu.roll(n, IDX_DIM - half, 1) * ib1_ref[...] + pltpu.roll(n, half, 1) * ib2_ref[...]
    ik_ref[...] = out.astype(BF16)
    glr_ref[...] = p[:, c0 + IDX_DIM:]


def _proj_k(h, w_k, tabs, seq, tm=512):
    t, d = h.shape
    nblk = seq // tm
    ncol = w_k.shape[1]
    tab_spec = pl.BlockSpec((tm, LANES), lambda i: (i % nblk, 0))
    return pl.pallas_call(
        _projk_body,
        grid=(t // tm,),
        in_specs=[pl.BlockSpec((tm, d), lambda i: (i, 0)), pl.BlockSpec((d, ncol), lambda i: (0, 0))] + [tab_spec] * 5,
        out_specs=[pl.BlockSpec((tm, DSA_KV_HEADS * HEAD_DIM), lambda i: (i, 0)),
                   pl.BlockSpec((tm, IDX_DIM), lambda i: (i, 0)),
                   pl.BlockSpec((tm, LANES), lambda i: (i, 0))],
        out_shape=[jax.ShapeDtypeStruct((t, DSA_KV_HEADS * HEAD_DIM), BF16),
                   jax.ShapeDtypeStruct((t, IDX_DIM), BF16),
                   jax.ShapeDtypeStruct((t, LANES), F32)],
        compiler_params=_params("parallel"),
        name="proj_k",
    )(h, w_k, *tabs)


def _projt_body(kind, w_ref, ht_ref, *rest):
    p = _dot(w_ref[...], ht_ref[...])
    o_ref = rest[-1]
    rows = p.shape[0]
    if kind == "v":
        o_ref[...] = p.astype(o_ref.dtype)
        return
    if kind == "iw":
        o_ref[...] = p * (IDX_HEADS ** -0.5)
        return
    ta_ref, tb_ref = rest[0], rest[1]
    for j in range(rows // HEAD_DIM):
        pj = p[j * HEAD_DIM:(j + 1) * HEAD_DIM]
        if kind == "dq":
            n = pj * lax.rsqrt(jnp.mean(pj * pj, axis=0, keepdims=True) + NORM_EPS)
            sw = jnp.concatenate([n[HEAD_DIM // 2:], n[:HEAD_DIM // 2]], axis=0)
        else:
            n = pj
            half = IDX_ROPE_DIM // 2
            sw = jnp.concatenate([n[half:2 * half], n[:half], n[2 * half:]], axis=0)
        o_ref[j * HEAD_DIM:(j + 1) * HEAD_DIM, :] = (n * ta_ref[...] + sw * tb_ref[...]).astype(o_ref.dtype)


def _proj_t(kind, w_t, ht, tabs, seq, out_dtype, tc, tt=512):
    c, d = w_t.shape
    t = ht.shape[1]
    nblk = seq // tt
    tab_spec = pl.BlockSpec((HEAD_DIM, tt), lambda i, j: (0, j % nblk))
    return pl.pallas_call(
        functools.partial(_projt_body, kind),
        grid=(c // tc, t // tt),
        in_specs=[pl.BlockSpec((tc, d), lambda i, j: (i, 0)), pl.BlockSpec((d, tt), lambda i, j: (0, j))]
        + [tab_spec] * len(tabs),
        out_specs=pl.BlockSpec((tc, tt), lambda i, j: (i, j)),
        out_shape=jax.ShapeDtypeStruct((c, t), out_dtype),
        compiler_params=_params("parallel", "parallel"),
        name="proj_t_" + kind,
    )(w_t, ht, *tabs)


def _gla_head(h, explicit, q_ref, k_ref, v_ref, gr_ref, ng_ref, o_ref, st_ref, b_scr):
    C, SB = GLA_CHUNK, GLA_SUB
    qc = slice(h * GLA_DK, (h + 1) * GLA_DK)
    vc = slice(h * GLA_DV, (h + 1) * GLA_DV)
    q = q_ref[:, qc] * (GLA_DK ** -0.5)
    vb = v_ref[:, vc].astype(BF16)
    b = b_scr[h]
    kt = k_ref[:, qc].T
    bt = b.T

    o = _dot((q * jnp.exp2(b)).astype(BF16), st_ref[h].astype(BF16))

    lane = lax.broadcasted_iota(jnp.int32, (SB, C), 1)
    rowi = lax.broadcasted_iota(jnp.int32, (SB, 1), 0)
    blocks = []
    for blk in range(C // SB):
        s = blk * SB
        q_i = q[s:s + SB]
        b_i = b[s:s + SB]
        if blk > 0:
            q_s = q_i * jnp.exp2(b_i - b_scr[h, s - 1:s, :])
            k_gap = bt[:, s - 1:s] - bt
        else:
            q_s = q_i * jnp.exp2(b_i)
            k_gap = -bt
        if explicit:
            a = jnp.zeros((SB, C), F32)
            for j in range(SB):
                r = s + j
                dec = jnp.where(rowi >= j, jnp.exp2(b_i - b_scr[h, r:r + 1, :]), 0.0)
                col = jnp.sum(q_i * k_ref[r:r + 1, qc] * dec, axis=-1, keepdims=True)
                a = jnp.where(lane == r, col, a)
            if blk > 0:
                k_s = kt * jnp.exp2(jnp.minimum(k_gap, 0.0))
                a = jnp.where(lane < s, _dot(q_s.astype(BF16), k_s.astype(BF16)), a)
        else:
            k_s = kt * jnp.exp2(jnp.minimum(k_gap, GLA_MAX_SPLIT_LOG2))
            a = jnp.where(lane <= s + rowi, _dot(q_s.astype(BF16), k_s.astype(BF16)), 0.0)
        blocks.append(a)
    o = o + _dot(jnp.concatenate(blocks, axis=0).astype(BF16), vb)

    b_last = bt[:, C - 1:C]
    k_dec = kt * jnp.exp2(b_last - bt)
    st_ref[h] = st_ref[h] * jnp.exp2(b_last) + _dot(k_dec.astype(BF16), vb)

    on = o * lax.rsqrt(jnp.mean(o * o, axis=-1, keepdims=True) + NORM_EPS) * ng_ref[...]
    gr = gr_ref[:, vc]
    o_ref[:, vc] = (on * gr * (1.0 / (1.0 + jnp.exp(-gr)))).astype(o_ref.dtype)


def _gla_body(q_ref, k_ref, v_ref, gr_ref, glr_ref, wg2_ref, bg_ref, ng_ref, o_ref, st_ref, b_scr):
    C, SB = GLA_CHUNK, GLA_SUB

    @pl.when(pl.program_id(1) == 0)
    def _():
        st_ref[...] = jnp.zeros_like(st_ref)

    z = _dot(glr_ref[...].astype(BF16), wg2_ref[...]) + bg_ref[...]
    g = (jnp.minimum(z, 0.0) - jnp.log1p(jnp.exp(-jnp.abs(z)))) * (LOG2E / GLA_GATE_NORMALIZER)
    ri = lax.broadcasted_iota(jnp.int32, (C, C), 0)
    ci = lax.broadcasted_iota(jnp.int32, (C, C), 1)
    tri = jnp.where(ci <= ri, 1.0, 0.0).astype(BF16)
    g1 = g.astype(BF16)
    r1 = g - g1.astype(F32)
    g2 = r1.astype(BF16)
    g3 = (r1 - g2.astype(F32)).astype(BF16)
    b = _dot(tri, g1) + _dot(tri, g2) + _dot(tri, g3)
    for h in range(GLA_HEADS):
        b_scr[h] = b[:, h * GLA_DK:(h + 1) * GLA_DK]

    drop = -b[SB - 1:SB, :]
    for blk in range(1, C // SB):
        s = blk * SB
        drop = jnp.maximum(drop, b[s - 1:s, :] - b[s + SB - 1:s + SB, :])
    mild = jnp.max(drop) <= GLA_MAX_SPLIT_LOG2

    refs = (q_ref, k_ref, v_ref, gr_ref, ng_ref, o_ref, st_ref, b_scr)

    @pl.when(mild)
    def _():
        for h in range(GLA_HEADS):
            _gla_head(h, False, *refs)

    @pl.when(jnp.logical_not(mild))
    def _():
        for h in range(GLA_HEADS):
            _gla_head(h, True, *refs)


def _gla(p_nat, glr, wg2p, bg, ng, batch, seq):
    C = GLA_CHUNK
    nc = seq // C
    t = batch * seq
    nqk = GLA_HEADS * GLA_DK
    nv = GLA_HEADS * GLA_DV
    vblk = 2 * nqk // nv
    row = lambda b, c: b * nc + c
    return pl.pallas_call(
        _gla_body,
        grid=(batch, nc),
        in_specs=[
            pl.BlockSpec((C, nqk), lambda b, c: (row(b, c), 0)),
            pl.BlockSpec((C, nqk), lambda b, c: (row(b, c), 1)),
            pl.BlockSpec((C, nv), lambda b, c: (row(b, c), vblk)),
            pl.BlockSpec((C, nv), lambda b, c: (row(b, c), vblk + 1)),
            pl.BlockSpec((C, LANES), lambda b, c: (row(b, c), 0)),
            pl.BlockSpec((LANES, nqk), lambda b, c: (0, 0)),
            pl.BlockSpec((1, nqk), lambda b, c: (0, 0)),
            pl.BlockSpec((1, GLA_DV), lambda b, c: (0, 0)),
        ],
        out_specs=pl.BlockSpec((C, nv), lambda b, c: (row(b, c), 0)),
        out_shape=jax.ShapeDtypeStruct((t, nv), BF16),
        scratch_shapes=[pltpu.VMEM((GLA_HEADS, GLA_DK, GLA_DV), F32), pltpu.VMEM((GLA_HEADS, C, GLA_DK), F32)],
        compiler_params=_params("parallel", "arbitrary"),
        name="gla",
    )(p_nat, p_nat, p_nat, p_nat, glr, wg2p, bg, ng)


def _dsa_body(topk, bound_ref, iqt_ref, iwt_ref, dqt_ref, ik_ref, dk_ref, vt_ref, o_ref,
              sc_ref, hi_ref, lo_ref, qg_ref, m_ref, l_ref, acc_ref):
    TQ, TK = DSA_TQ, DSA_TK
    i = pl.program_id(1)
    nk = (i * TQ + TQ + TK - 1) // TK
    q_pos = i * TQ + lax.broadcasted_iota(jnp.int32, (TK, TQ), 1)
    k_off = lax.broadcasted_iota(jnp.int32, (TK, TQ), 0)

    def score_tile(j, carry):
        base = pl.multiple_of(j * TK, TK)
        ikt = ik_ref[pl.ds(base, TK), :]
        acc = jnp.zeros((TK, TQ), F32)
        for h in range(IDX_HEADS):
            r = _dot(ikt, iqt_ref[h * IDX_DIM:(h + 1) * IDX_DIM, :])
            acc = acc + jnp.maximum(r, 0.0) * iwt_ref[h:h + 1, :]
        bits = pltpu.bitcast(acc, jnp.int32)
        key = jnp.where(bits < 0, bits ^ jnp.int32(0x7FFFFFFF), bits)
        key = jnp.where(base + k_off <= q_pos, key, jnp.int32(INT_MIN))
        sc_ref[pl.ds(base, TK), :] = key
        hi_ref[pl.ds(base, TK), :] = jnp.right_shift(key, 16).astype(jnp.int16)
        lo_ref[pl.ds(base, TK), :] = ((key & jnp.int32(0xFFFF)) - jnp.int32(HALF_BIAS)).astype(jnp.int16)
        return carry

    lax.fori_loop(0, nk, score_tile, 0)

    CR = DSA_COUNT_ROWS

    def count_ge16(ref, cand16):
        def body(j, cnt):
            base = pl.multiple_of(j * TK, TK)
            hit = jnp.where(ref[pl.ds(base, TK), :] >= cand16, jnp.int16(1), jnp.int16(0))
            for r in range(TK // CR):
                cnt = cnt + hit[r * CR:(r + 1) * CR]
            return cnt
        cnt = lax.fori_loop(0, nk, body, jnp.zeros((CR, TQ), jnp.int16))
        return jnp.sum(cnt.astype(jnp.int32), axis=0, keepdims=True)

    def kth_largest16(ref, rank):
        zero = jnp.zeros((1, TQ), jnp.int32)
        t0 = jnp.where(count_ge16(ref, zero.astype(jnp.int16)) >= rank, zero, jnp.int32(-HALF_BIAS))

        def bit_step(it, t):
            cand = t | jnp.left_shift(jnp.int32(1), 14 - it)
            return jnp.where(count_ge16(ref, cand.astype(jnp.int16)) >= rank, cand, t)

        return lax.fori_loop(0, 15, bit_step, t0)

    t_hi = kth_largest16(hi_ref, topk)
    above = count_ge16(hi_ref, jnp.minimum(t_hi + 1, HALF_BIAS - 1).astype(jnp.int16))
    above = jnp.where(t_hi >= HALF_BIAS - 1, 0, above)
    t_hi16 = t_hi.astype(jnp.int16)

    def low_tile(j, carry):
        base = pl.multiple_of(j * TK, TK)
        rows = pl.ds(base, TK)
        lo_ref[rows, :] = jnp.where(hi_ref[rows, :] == t_hi16, lo_ref[rows, :], jnp.int16(-HALF_BIAS))
        return carry

    lax.fori_loop(0, nk, low_tile, 0)
    t_lo = kth_largest16(lo_ref, topk - above)
    thr = jnp.left_shift(t_hi, 16) + (t_lo + HALF_BIAS)
    thr = jnp.maximum(thr, jnp.int32(INT_MIN + 1))

    grp = DSA_HEADS // DSA_KV_HEADS
    AK = DSA_ATK
    na = (i * TQ + TQ + AK - 1) // AK
    for g in range(DSA_KV_HEADS):
        qg_ref[g] = jnp.concatenate(
            [dqt_ref[(g * grp + hh) * HEAD_DIM:(g * grp + hh + 1) * HEAD_DIM, :] for hh in range(grp)], axis=1)

    def tile_operands(j, g):
        base = pl.multiple_of(j * AK, AK)
        kt = dk_ref[pl.ds(base, AK), g * HEAD_DIM:(g + 1) * HEAD_DIM]
        vt = vt_ref[g * HEAD_DIM:(g + 1) * HEAD_DIM, pl.ds(base, AK)]
        return kt, vt

    def tile_bias(j, selected):
        base = pl.multiple_of(j * AK, AK)
        bias = jnp.where(sc_ref[pl.ds(base, AK), :] >= thr, selected, NEG_BIG)
        return jnp.concatenate([bias] * grp, axis=1)

    l_ref[...] = jnp.zeros_like(l_ref)
    acc_ref[...] = jnp.zeros_like(acc_ref)
    neg_bound = -bound_ref[0, 0]

    def bounded_tile(j, carry):
        bias = tile_bias(j, neg_bound)
        for g in range(DSA_KV_HEADS):
            kt, vt = tile_operands(j, g)
            p = jnp.exp2(_dot(kt, qg_ref[g]) + bias)
            l_ref[g] += jnp.sum(p, axis=0, keepdims=True)
            acc_ref[g] += _dot(vt, p.astype(BF16))
        return carry

    lax.fori_loop(0, na, bounded_tile, 0)

    @pl.when(jnp.logical_not(jnp.min(l_ref[...]) >= DSA_MIN_DENOM))
    def _():
        m_ref[...] = jnp.full_like(m_ref, NEG_BIG)
        l_ref[...] = jnp.zeros_like(l_ref)
        acc_ref[...] = jnp.zeros_like(acc_ref)

        def online_tile(j, carry):
            bias = tile_bias(j, 0.0)
            for g in range(DSA_KV_HEADS):
                kt, vt = tile_operands(j, g)
                s = _dot(kt, qg_ref[g]) + bias
                m_old = m_ref[g]
                m_new = jnp.maximum(m_old, jnp.max(s, axis=0, keepdims=True))
                p = jnp.exp2(s - m_new)
                alpha = jnp.exp2(m_old - m_new)
                l_ref[g] = alpha * l_ref[g] + jnp.sum(p, axis=0, keepdims=True)
                m_ref[g] = m_new
                acc_ref[g] = alpha * acc_ref[g] + _dot(vt, p.astype(BF16))
            return carry

        lax.fori_loop(0, na, online_tile, 0)

    for g in range(DSA_KV_HEADS):
        o_g = acc_ref[g] * (1.0 / l_ref[g])
        for hh in range(grp):
            h = g * grp + hh
            o_ref[:, h * HEAD_DIM:(h + 1) * HEAD_DIM] = o_g[:, hh * TQ:(hh + 1) * TQ].T.astype(o_ref.dtype)


def _dsa(bound, iqt, iwt, dqt, ik, dk, vt, batch, seq, topk):
    TQ = DSA_TQ
    nq = seq // TQ
    t = batch * seq
    nh = DSA_HEADS * HEAD_DIM
    nkv = DSA_KV_HEADS * HEAD_DIM
    grp = DSA_HEADS // DSA_KV_HEADS
    col = lambda b, i: (0, b * nq + i)
    return pl.pallas_call(
        functools.partial(_dsa_body, topk),
        grid=(batch, nq),
        in_specs=[
            pl.BlockSpec(memory_space=pltpu.SMEM),
            pl.BlockSpec((IDX_HEADS * IDX_DIM, TQ), col),
            pl.BlockSpec((IDX_HEADS, TQ), col),
            pl.BlockSpec((nh, TQ), col),
            pl.BlockSpec((seq, IDX_DIM), lambda b, i: (b, 0), pipeline_mode=pl.Buffered(1)),
            pl.BlockSpec((seq, nkv), lambda b, i: (b, 0), pipeline_mode=pl.Buffered(1)),
            pl.BlockSpec((nkv, seq), lambda b, i: (0, b), pipeline_mode=pl.Buffered(1)),
        ],
        out_specs=pl.BlockSpec((TQ, nh), lambda b, i: (b * nq + i, 0)),
        out_shape=jax.ShapeDtypeStruct((t, nh), BF16),
        scratch_shapes=[
            pltpu.VMEM((seq, TQ), jnp.int32),
            pltpu.VMEM((seq, TQ), jnp.int16),
            pltpu.VMEM((seq, TQ), jnp.int16),
            pltpu.VMEM((DSA_KV_HEADS, HEAD_DIM, grp * TQ), BF16),
            pltpu.VMEM((DSA_KV_HEADS, 1, grp * TQ), F32),
            pltpu.VMEM((DSA_KV_HEADS, 1, grp * TQ), F32),
            pltpu.VMEM((DSA_KV_HEADS, HEAD_DIM, grp * TQ), F32),
        ],
        compiler_params=_params("parallel", "arbitrary"),
        name="dsa",
    )(bound, iqt, iwt, dqt, ik, dk, vt)


def _merge_body(og_ref, od_ref, wg_ref, wd_ref, gg_ref, gd_ref, bgg_ref, bgd_ref, o_ref):
    yg = _dot(og_ref[...], wg_ref[...])
    yd = _dot(od_ref[...], wd_ref[...])
    sg = 1.0 / (1.0 + jnp.exp(-(gg_ref[...] + bgg_ref[...])))
    sd = 1.0 / (1.0 + jnp.exp(-(gd_ref[...] + bgd_ref[...])))
    o_ref[...] = (sg * yg + sd * yd).astype(o_ref.dtype)


def _merge(o_gla, o_dsa, wg, wd, p_nat, b_gate, gate_col, tm=1024, tn=512):
    t, dg = o_gla.shape
    dd = o_dsa.shape[1]
    d = wg.shape[1]
    gb = gate_col // tn
    nb = d // tn
    return pl.pallas_call(
        _merge_body,
        grid=(t // tm, nb),
        in_specs=[
            pl.BlockSpec((tm, dg), lambda i, j: (i, 0)),
            pl.BlockSpec((tm, dd), lambda i, j: (i, 0)),
            pl.BlockSpec((dg, tn), lambda i, j: (0, j)),
            pl.BlockSpec((dd, tn), lambda i, j: (0, j)),
            pl.BlockSpec((tm, tn), lambda i, j: (i, gb + j)),
            pl.BlockSpec((tm, tn), lambda i, j: (i, gb + nb + j)),
            pl.BlockSpec((1, tn), lambda i, j: (0, j)),
            pl.BlockSpec((1, tn), lambda i, j: (0, nb + j)),
        ],
        out_specs=pl.BlockSpec((tm, tn), lambda i, j: (i, j)),
        out_shape=jax.ShapeDtypeStruct((t, d), BF16),
        compiler_params=_params("parallel", "parallel"),
        name="merge",
    )(o_gla, o_dsa, wg, wd, p_nat, p_nat, b_gate, b_gate)


def _outproj_body(x_ref, m_ref, w_ref, o_ref):
    o_ref[...] = x_ref[...] + _dot(m_ref[...], w_ref[...])


def _outproj(x2, m, w_out, tm=1024, tn=1024):
    t, d = x2.shape
    tn = min(tn, d)
    return pl.pallas_call(
        _outproj_body,
        grid=(t // tm, d // tn),
        in_specs=[
            pl.BlockSpec((tm, tn), lambda i, j: (i, j)),
            pl.BlockSpec((tm, d), lambda i, j: (i, 0)),
            pl.BlockSpec((d, tn), lambda i, j: (0, j)),
        ],
        out_specs=pl.BlockSpec((tm, tn), lambda i, j: (i, j)),
        out_shape=jax.ShapeDtypeStruct((t, d), F32),
        compiler_params=_params("parallel", "parallel"),
        name="out_proj",
    )(x2, m, w_out)


def _mlp_body(x_ref, g_ref, w1_ref, w2_ref, o_ref, h_scr):
    @pl.when(pl.program_id(1) == 0)
    def _():
        x = x_ref[...]
        h_scr[...] = (x * lax.rsqrt(jnp.mean(x * x, axis=-1, keepdims=True) + NORM_EPS) * g_ref[...]).astype(BF16)
        o_ref[...] = x

    u = jnp.maximum(_dot(h_scr[...], w1_ref[...]), 0.0)
    o_ref[...] += _dot((u * u).astype(BF16), w2_ref[...])


def _mlp(x1, g, w1, w2, tm=512, tf=1024):
    t, d = x1.shape
    ff = w1.shape[1]
    return pl.pallas_call(
        _mlp_body,
        grid=(t // tm, ff // tf),
        in_specs=[
            pl.BlockSpec((tm, d), lambda i, f: (i, 0)),
            pl.BlockSpec((1, d), lambda i, f: (0, 0)),
            pl.BlockSpec((d, tf), lambda i, f: (0, f)),
            pl.BlockSpec((tf, d), lambda i, f: (f, 0)),
        ],
        out_specs=pl.BlockSpec((tm, d), lambda i, f: (i, 0)),
        out_shape=jax.ShapeDtypeStruct((t, d), F32),
        scratch_shapes=[pltpu.VMEM((tm, d), BF16)],
        compiler_params=_params("parallel", "arbitrary"),
        name="mlp",
    )(x1, g.reshape(1, d), w1, w2)


def _rope_tables(seq, q_g, k_g, ik_g):
    pos = jnp.arange(seq, dtype=jnp.int32).astype(F32)

    def cs(rot):
        half = rot // 2
        inv = ROPE_THETA ** (-jnp.arange(half, dtype=F32) * 2.0 / rot)
        ang = pos[:, None] * inv[None, :]
        return jnp.cos(ang), jnp.sin(ang)

    c, s = cs(HEAD_DIM)
    hh = HEAD_DIM // 2
    ka = jnp.concatenate([c, c], axis=1) * k_g[None, :]
    kb = jnp.concatenate([-s * k_g[None, hh:], s * k_g[None, :hh]], axis=1)
    qs = (HEAD_DIM ** -0.5) * LOG2E
    qa = (jnp.concatenate([c, c], axis=1) * q_g[None, :] * qs).T
    qb = (jnp.concatenate([-s * q_g[None, hh:], s * q_g[None, :hh]], axis=1) * qs).T

    ci, si = cs(IDX_ROPE_DIM)
    h2 = IDX_ROPE_DIM // 2
    one = jnp.ones((seq, IDX_DIM - IDX_ROPE_DIM), F32)
    zero = jnp.zeros((seq, IDX_DIM - IDX_ROPE_DIM), F32)
    zh = jnp.zeros((seq, h2), F32)
    ia = jnp.concatenate([ci, ci, one], axis=1) * ik_g[None, :]
    ib1 = jnp.concatenate([-si * ik_g[None, h2:2 * h2], zh, zero], axis=1)
    ib2 = jnp.concatenate([zh, si * ik_g[None, :h2], zero], axis=1)
    iqs = IDX_DIM ** -0.5
    iqa = (jnp.concatenate([ci, ci, one], axis=1) * iqs).T
    iqb = (jnp.concatenate([-si, si, zero], axis=1) * iqs).T
    return (ka, kb, ia, ib1, ib2), (qa, qb), (iqa, iqb)


def kernel(x, norm1_g, w_in, gla_wg2, gla_bg, gla_norm_g, w_proj_gla, q_norm_g, k_norm_g, idx_k_norm_g,
           w_proj_dsa, b_gate, w_out, norm2_g, w_ff1, w_ff2):
    batch, seq, d = x.shape
    depth = w_in.shape[0]
    topk = min(TOPK_MAX, seq // 4)
    nqk = GLA_HEADS * GLA_DK
    nv = GLA_HEADS * GLA_DV
    ndq = DSA_HEADS * HEAD_DIM
    nkv = DSA_KV_HEADS * HEAD_DIM
    niq = IDX_HEADS * IDX_DIM
    splits = (nqk, nqk, nv, nv, GLA_GATE_RANK, ndq, nkv, nkv, niq, IDX_DIM, IDX_HEADS, 2 * d)
    offs = [0]
    for n in splits:
        offs.append(offs[-1] + n)
    o_gq, _, _, _, o_glr, o_dq, o_dk, o_dv, o_iq, o_ik, o_iw, o_gate, _ = offs
    n_gla_cols = 2 * nqk + 2 * nv
    assert n_gla_cols % 512 == 0 and seq % 512 == 0 and d % 512 == 0

    x2 = x.reshape(batch * seq, d)
    for l in range(depth):
        w = w_in[l]
        w_nat = jnp.concatenate([w[:, o_gq:o_glr], w[:, o_gate:]], axis=1).astype(BF16)
        glr_pad = jnp.zeros((d, LANES - GLA_GATE_RANK), F32)
        w_k = jnp.concatenate([w[:, o_dk:o_dv], w[:, o_ik:o_iw], w[:, o_glr:o_dq], glr_pad], axis=1).astype(BF16)
        wt_dq = w[:, o_dq:o_dk].T.astype(BF16)
        wt_iq = w[:, o_iq:o_ik].T.astype(BF16)
        wt_v = w[:, o_dv:o_iq].T.astype(BF16)
        wt_iw = w[:, o_iw:o_gate].T.astype(BF16)
        wg2p = jnp.concatenate(
            [gla_wg2[l], jnp.zeros((LANES - GLA_GATE_RANK, nqk), F32)], axis=0).astype(BF16)
        k_tabs, q_tabs, iq_tabs = _rope_tables(seq, q_norm_g[l], k_norm_g[l], idx_k_norm_g[l])

        h, ht = _rmsnorm_both(x2, norm1_g[l])
        p_nat = _mm(h, w_nat, F32, 1024, 1024, "proj_nat")
        dk, ik, glr = _proj_k(h, w_k, k_tabs, seq)
        dqt = _proj_t("dq", wt_dq, ht, q_tabs, seq, BF16, 1024)
        iqt = _proj_t("iq", wt_iq, ht, iq_tabs, seq, BF16, 1024)
        vt = _proj_t("v", wt_v, ht, (), seq, BF16, nkv)
        iwt = _proj_t("iw", wt_iw, ht, (), seq, F32, IDX_HEADS)

        o_gla = _gla(p_nat, glr, wg2p, gla_bg[l].reshape(1, nqk), gla_norm_g[l].reshape(1, GLA_DV), batch, seq)
        bound = (1.02 * HEAD_DIM * (HEAD_DIM ** -0.5) * LOG2E
                 * jnp.max(jnp.abs(q_norm_g[l])) * jnp.max(jnp.abs(k_norm_g[l]))).reshape(1, 1).astype(F32)
        o_dsa = _dsa(bound, iqt, iwt, dqt, ik, dk, vt, batch, seq, topk)

        m = _merge(o_gla, o_dsa, w_proj_gla[l].astype(BF16), w_proj_dsa[l].astype(BF16), p_nat,
                   b_gate[l].reshape(1, 2 * d), n_gla_cols)
        x1 = _outproj(x2, m, w_out[l].astype(BF16))
        x2 = _mlp(x1, norm2_g[l], w_ff1[l].astype(BF16), w_ff2[l].astype(BF16))
    return x2.reshape(batch, seq, d)
```

```python
import functools
import math

import jax
import jax.numpy as jnp
from jax import lax
from jax.experimental import pallas as pl
from jax.experimental.pallas import tpu as pltpu

BF16 = jnp.bfloat16
F32 = jnp.float32

NORM_EPS = 1e-6
ROPE_THETA = 10000.0
GLA_HEADS = 4
GLA_DK = 256
GLA_DV = 512
GLA_GATE_RANK = 16
GLA_GATE_NORMALIZER = 16.0
DSA_HEADS = 16
DSA_KV_HEADS = 4
HEAD_DIM = 128
IDX_HEADS = 16
IDX_DIM = 128
IDX_ROPE_DIM = 64
TOPK_MAX = 256

LANES = 128
VMEM_LIMIT = 56 * 1024 * 1024
GLA_CHUNK = 128
GLA_SUB = 16
GLA_MAX_SPLIT_LOG2 = 100.0
DSA_TQ = 256
DSA_TK = 512
DSA_ATK = 512
DSA_COUNT_ROWS = 64
DSA_MIN_DENOM = 2.0 ** -80
INT_MIN = -(2 ** 31)
HALF_BIAS = 2 ** 15
NEG_BIG = -1e30
LOG2E = math.log2(math.e)


def _loop_by_pairs(n, body):
    def pair(jj, carry):
        body(2 * jj)
        body(2 * jj + 1)
        return carry

    lax.fori_loop(0, n // 2, pair, 0)

    @pl.when(n % 2 == 1)
    def _():
        body(n - 1)


def _params(*sem):
    return pltpu.CompilerParams(dimension_semantics=sem, vmem_limit_bytes=VMEM_LIMIT)


def _dot(a, b):
    return jnp.dot(a, b, preferred_element_type=F32)


def _rms_body(x_ref, g_ref, h_ref, ht_ref):
    x = x_ref[...]
    y = x * lax.rsqrt(jnp.mean(x * x, axis=-1, keepdims=True) + NORM_EPS) * g_ref[...]
    h_ref[...] = y.astype(BF16)
    ht_ref[...] = y.T.astype(BF16)


def _rmsnorm_both(x2, g, tm=256):
    t, d = x2.shape
    return pl.pallas_call(
        _rms_body,
        grid=(t // tm,),
        in_specs=[pl.BlockSpec((tm, d), lambda i: (i, 0)), pl.BlockSpec((1, d), lambda i: (0, 0))],
        out_specs=[pl.BlockSpec((tm, d), lambda i: (i, 0)), pl.BlockSpec((d, tm), lambda i: (0, i))],
        out_shape=[jax.ShapeDtypeStruct((t, d), BF16), jax.ShapeDtypeStruct((d, t), BF16)],
        compiler_params=_params("parallel"),
        name="rmsnorm_in",
    )(x2, g.reshape(1, d))


def _mm_body(a_ref, w_ref, o_ref):
    o_ref[...] = _dot(a_ref[...], w_ref[...]).astype(o_ref.dtype)


def _mm(a, w, out_dtype, tm, tn, name):
    m, k = a.shape
    n = w.shape[1]
    return pl.pallas_call(
        _mm_body,
        grid=(m // tm, n // tn),
        in_specs=[pl.BlockSpec((tm, k), lambda i, j: (i, 0)), pl.BlockSpec((k, tn), lambda i, j: (0, j))],
        out_specs=pl.BlockSpec((tm, tn), lambda i, j: (i, j)),
        out_shape=jax.ShapeDtypeStruct((m, n), out_dtype),
        compiler_params=_params("parallel", "parallel"),
        name=name,
    )(a, w)


def _projk_body(h_ref, w_ref, ka_ref, kb_ref, ia_ref, ib1_ref, ib2_ref, dk_ref, ik_ref, glr_ref):
    p = _dot(h_ref[...], w_ref[...])
    nkv = DSA_KV_HEADS
    for j in range(nkv):
        pj = p[:, j * HEAD_DIM:(j + 1) * HEAD_DIM]
        n = pj * lax.rsqrt(jnp.mean(pj * pj, axis=-1, keepdims=True) + NORM_EPS)
        out = n * ka_ref[...] + pltpu.roll(n, HEAD_DIM // 2, 1) * kb_ref[...]
        dk_ref[:, j * HEAD_DIM:(j + 1) * HEAD_DIM] = out.astype(BF16)
    c0 = nkv * HEAD_DIM
    pi = p[:, c0:c0 + IDX_DIM]
    n = pi * lax.rsqrt(jnp.mean(pi * pi, axis=-1, keepdims=True) + NORM_EPS)
    half = IDX_ROPE_DIM // 2
    out = n * ia_ref[...] + pltpu.roll(n, IDX_DIM - half, 1) * ib1_ref[...] + pltpu.roll(n, half, 1) * ib2_ref[...]
    ik_ref[...] = out.astype(BF16)
    glr_ref[...] = p[:, c0 + IDX_DIM:]


def _proj_k(h, w_k, tabs, seq, tm=512):
    t, d = h.shape
    nblk = seq // tm
    ncol = w_k.shape[1]
    tab_spec = pl.BlockSpec((tm, LANES), lambda i: (i % nblk, 0))
    return pl.pallas_call(
        _projk_body,
        grid=(t // tm,),
        in_specs=[pl.BlockSpec((tm, d), lambda i: (i, 0)), pl.BlockSpec((d, ncol), lambda i: (0, 0))] + [tab_spec] * 5,
        out_specs=[pl.BlockSpec((tm, DSA_KV_HEADS * HEAD_DIM), lambda i: (i, 0)),
                   pl.BlockSpec((tm, IDX_DIM), lambda i: (i, 0)),
                   pl.BlockSpec((tm, LANES), lambda i: (i, 0))],
        out_shape=[jax.ShapeDtypeStruct((t, DSA_KV_HEADS * HEAD_DIM), BF16),
                   jax.ShapeDtypeStruct((t, IDX_DIM), BF16),
                   jax.ShapeDtypeStruct((t, LANES), F32)],
        compiler_params=_params("parallel"),
        name="proj_k",
    )(h, w_k, *tabs)


def _projt_body(kind, w_ref, ht_ref, *rest):
    p = _dot(w_ref[...], ht_ref[...])
    o_ref = rest[-1]
    rows = p.shape[0]
    if kind == "v":
        o_ref[...] = p.astype(o_ref.dtype)
        return
    if kind == "iw":
        o_ref[...] = p * (IDX_HEADS ** -0.5)
        return
    ta_ref, tb_ref = rest[0], rest[1]
    for j in range(rows // HEAD_DIM):
        pj = p[j * HEAD_DIM:(j + 1) * HEAD_DIM]
        if kind == "dq":
            n = pj * lax.rsqrt(jnp.mean(pj * pj, axis=0, keepdims=True) + NORM_EPS)
            sw = jnp.concatenate([n[HEAD_DIM // 2:], n[:HEAD_DIM // 2]], axis=0)
        else:
            n = pj
            half = IDX_ROPE_DIM // 2
            sw = jnp.concatenate([n[half:2 * half], n[:half], n[2 * half:]], axis=0)
        o_ref[j * HEAD_DIM:(j + 1) * HEAD_DIM, :] = (n * ta_ref[...] + sw * tb_ref[...]).astype(o_ref.dtype)


def _proj_t(kind, w_t, ht, tabs, seq, out_dtype, tc, tt=512):
    c, d = w_t.shape
    t = ht.shape[1]
    nblk = seq // tt
    tab_spec = pl.BlockSpec((HEAD_DIM, tt), lambda i, j: (0, j % nblk))
    return pl.pallas_call(
        functools.partial(_projt_body, kind),
        grid=(c // tc, t // tt),
        in_specs=[pl.BlockSpec((tc, d), lambda i, j: (i, 0)), pl.BlockSpec((d, tt), lambda i, j: (0, j))]
        + [tab_spec] * len(tabs),
        out_specs=pl.BlockSpec((tc, tt), lambda i, j: (i, j)),
        out_shape=jax.ShapeDtypeStruct((c, t), out_dtype),
        compiler_params=_params("parallel", "parallel"),
        name="proj_t_" + kind,
    )(w_t, ht, *tabs)


def _gla_scores_explicit(q, k, b, bt):
    C, SB = GLA_CHUNK, GLA_SUB
    kt = k.T
    lane = lax.broadcasted_iota(jnp.int32, (SB, C), 1)
    rowi = lax.broadcasted_iota(jnp.int32, (SB, 1), 0)
    blocks = []
    for blk in range(C // SB):
        s = blk * SB
        q_i = q[s:s + SB]
        b_i = b[s:s + SB]
        a = jnp.zeros((SB, C), F32)
        for j in range(SB):
            r = s + j
            dec = jnp.where(rowi >= j, jnp.exp2(b_i - b[r:r + 1, :]), 0.0)
            col = jnp.sum(q_i * k[r:r + 1, :] * dec, axis=-1, keepdims=True)
            a = jnp.where(lane == r, col, a)
        if blk > 0:
            q_s = q_i * jnp.exp2(b_i - b[s - 1:s, :])
            k_s = kt * jnp.exp2(jnp.minimum(bt[:, s - 1:s] - bt, 0.0))
            a = jnp.where(lane < s, _dot(q_s.astype(BF16), k_s.astype(BF16)), a)
        blocks.append(a)
    return jnp.concatenate(blocks, axis=0)


def _gla_scores_split(q, k, b):
    C, SB = GLA_CHUNK, GLA_SUB
    nt = (((1,), (1,)), ((), ()))
    ri = lax.broadcasted_iota(jnp.int32, (C, C), 0)
    ci = lax.broadcasted_iota(jnp.int32, (C, C), 1)
    zeros = lambda n: jnp.zeros((n, GLA_DK), F32)
    a = jnp.zeros((C, C), F32)
    span = C // 2
    while span >= SB:
        q_parts, k_parts = [], []
        for lo in range(0, C, 2 * span):
            mid, hi = lo + span, lo + 2 * span
            ref = b[mid - 1:mid, :]
            k_parts += [k[lo:mid] * jnp.exp2(ref - b[lo:mid]), zeros(span)]
            q_parts += [zeros(span), q[mid:hi] * jnp.exp2(b[mid:hi] - ref)]
        s_l = lax.dot_general(jnp.concatenate(q_parts, axis=0).astype(BF16),
                              jnp.concatenate(k_parts, axis=0).astype(BF16), nt, preferred_element_type=F32)
        shift = (2 * span).bit_length() - 1
        a = a + jnp.where(jnp.right_shift(ri, shift) == jnp.right_shift(ci, shift), s_l, 0.0)
        span //= 2
    q_parts, k_parts = [], []
    for lo in range(0, C, SB):
        gap = b[lo:lo + SB] - b[lo - 1:lo, :] if lo > 0 else b[lo:lo + SB]
        q_parts.append(q[lo:lo + SB] * jnp.exp2(gap))
        k_parts.append(k[lo:lo + SB] * jnp.exp2(jnp.minimum(-gap, GLA_MAX_SPLIT_LOG2)))
    s_d = lax.dot_general(jnp.concatenate(q_parts, axis=0).astype(BF16),
                          jnp.concatenate(k_parts, axis=0).astype(BF16), nt, preferred_element_type=F32)
    shift = SB.bit_length() - 1
    same = jnp.right_shift(ri, shift) == jnp.right_shift(ci, shift)
    return a + jnp.where(same, jnp.where(ci <= ri, s_d, 0.0), 0.0)


def _gla_head(h, explicit, q_ref, k_ref, v_ref, gr_ref, ng_ref, o_ref, st_ref, b_scr):
    C = GLA_CHUNK
    qc = slice(h * GLA_DK, (h + 1) * GLA_DK)
    vc = slice(h * GLA_DV, (h + 1) * GLA_DV)
    q = q_ref[:, qc] * (GLA_DK ** -0.5)
    k = k_ref[:, qc]
    vb = v_ref[:, vc].astype(BF16)
    b = b_scr[h]
    bt = b.T

    o = _dot((q * jnp.exp2(b)).astype(BF16), st_ref[h].astype(BF16))
    a = _gla_scores_explicit(q, k, b, bt) if explicit else _gla_scores_split(q, k, b)
    o = o + _dot(a.astype(BF16), vb)

    k_dec = k * jnp.exp2(b[C - 1:C, :] - b)
    st_ref[h] = st_ref[h] * jnp.exp2(bt[:, C - 1:C]) + _dot(k_dec.T.astype(BF16), vb)

    on = o * lax.rsqrt(jnp.mean(o * o, axis=-1, keepdims=True) + NORM_EPS) * ng_ref[...]
    gr = gr_ref[:, vc]
    o_ref[:, vc] = (on * gr * (1.0 / (1.0 + jnp.exp(-gr)))).astype(o_ref.dtype)


def _gla_body(q_ref, k_ref, v_ref, gr_ref, glr_ref, wg2_ref, bg_ref, ng_ref, o_ref, st_ref, b_scr):
    C, SB = GLA_CHUNK, GLA_SUB

    @pl.when(pl.program_id(1) == 0)
    def _():
        st_ref[...] = jnp.zeros_like(st_ref)

    z = _dot(glr_ref[...].astype(BF16), wg2_ref[...]) + bg_ref[...]
    g = (jnp.minimum(z, 0.0) - jnp.log1p(jnp.exp(-jnp.abs(z)))) * (LOG2E / GLA_GATE_NORMALIZER)
    ri = lax.broadcasted_iota(jnp.int32, (C, C), 0)
    ci = lax.broadcasted_iota(jnp.int32, (C, C), 1)
    tri = jnp.where(ci <= ri, 1.0, 0.0).astype(BF16)
    g1 = g.astype(BF16)
    r1 = g - g1.astype(F32)
    g2 = r1.astype(BF16)
    g3 = (r1 - g2.astype(F32)).astype(BF16)
    b = _dot(tri, g1) + _dot(tri, g2) + _dot(tri, g3)
    for h in range(GLA_HEADS):
        b_scr[h] = b[:, h * GLA_DK:(h + 1) * GLA_DK]

    drop = -b[SB - 1:SB, :]
    for blk in range(1, C // SB):
        s = blk * SB
        drop = jnp.maximum(drop, b[s - 1:s, :] - b[s + SB - 1:s + SB, :])
    mild = jnp.max(drop) <= GLA_MAX_SPLIT_LOG2

    refs = (q_ref, k_ref, v_ref, gr_ref, ng_ref, o_ref, st_ref, b_scr)

    @pl.when(mild)
    def _():
        for h in range(GLA_HEADS):
            _gla_head(h, False, *refs)

    @pl.when(jnp.logical_not(mild))
    def _():
        for h in range(GLA_HEADS):
            _gla_head(h, True, *refs)


def _gla(p_gla, glr, wg2p, bg, ng, batch, seq):
    C = GLA_CHUNK
    nc = seq // C
    t = batch * seq
    nqk = GLA_HEADS * GLA_DK
    nv = GLA_HEADS * GLA_DV
    vblk = 2 * nqk // nv
    row = lambda b, c: b * nc + c
    return pl.pallas_call(
        _gla_body,
        grid=(batch, nc),
        in_specs=[
            pl.BlockSpec((C, nqk), lambda b, c: (row(b, c), 0)),
            pl.BlockSpec((C, nqk), lambda b, c: (row(b, c), 1)),
            pl.BlockSpec((C, nv), lambda b, c: (row(b, c), vblk)),
            pl.BlockSpec((C, nv), lambda b, c: (row(b, c), vblk + 1)),
            pl.BlockSpec((C, LANES), lambda b, c: (row(b, c), 0)),
            pl.BlockSpec((LANES, nqk), lambda b, c: (0, 0)),
            pl.BlockSpec((1, nqk), lambda b, c: (0, 0)),
            pl.BlockSpec((1, GLA_DV), lambda b, c: (0, 0)),
        ],
        out_specs=pl.BlockSpec((C, nv), lambda b, c: (row(b, c), 0)),
        out_shape=jax.ShapeDtypeStruct((t, nv), BF16),
        scratch_shapes=[pltpu.VMEM((GLA_HEADS, GLA_DK, GLA_DV), F32), pltpu.VMEM((GLA_HEADS, C, GLA_DK), F32)],
        compiler_params=_params("parallel", "arbitrary"),
        name="gla",
    )(p_gla, p_gla, p_gla, p_gla, glr, wg2p, bg, ng)


def _dsa_body(topk, bound_ref, iqt_ref, iwt_ref, dqt_ref, ik_ref, dk_ref, vt_ref, o_ref,
              sc_ref, hi_ref, lo_ref, qg_ref, m_ref, l_ref, acc_ref):
    TQ, TK = DSA_TQ, DSA_TK
    i = pl.program_id(1)
    nk = (i * TQ + TQ + TK - 1) // TK
    q_pos = i * TQ + lax.broadcasted_iota(jnp.int32, (TK, TQ), 1)
    k_off = lax.broadcasted_iota(jnp.int32, (TK, TQ), 0)

    def score_tile(j, carry):
        base = pl.multiple_of(j * TK, TK)
        ikt = ik_ref[pl.ds(base, TK), :]
        acc = jnp.zeros((TK, TQ), F32)
        for h in range(IDX_HEADS):
            r = _dot(ikt, iqt_ref[h * IDX_DIM:(h + 1) * IDX_DIM, :])
            acc = acc + jnp.maximum(r, 0.0) * iwt_ref[h:h + 1, :]
        bits = pltpu.bitcast(acc, jnp.int32)
        key = jnp.where(bits < 0, bits ^ jnp.int32(0x7FFFFFFF), bits)
        key = jnp.where(base + k_off <= q_pos, key, jnp.int32(INT_MIN))
        sc_ref[pl.ds(base, TK), :] = key
        hi_ref[pl.ds(base, TK), :] = jnp.right_shift(key, 16).astype(jnp.int16)
        lo_ref[pl.ds(base, TK), :] = ((key & jnp.int32(0xFFFF)) - jnp.int32(HALF_BIAS)).astype(jnp.int16)
        return carry

    _loop_by_pairs(nk, lambda j: score_tile(j, 0))

    CR = DSA_COUNT_ROWS

    def count_ge16(ref, cand16):
        def body(j, cnt):
            base = pl.multiple_of(j * TK, TK)
            hit = jnp.where(ref[pl.ds(base, TK), :] >= cand16, jnp.int16(1), jnp.int16(0))
            for r in range(TK // CR):
                cnt = cnt + hit[r * CR:(r + 1) * CR]
            return cnt
        cnt = lax.fori_loop(0, nk, body, jnp.zeros((CR, TQ), jnp.int16))
        return jnp.sum(cnt.astype(jnp.int32), axis=0, keepdims=True)

    def kth_largest16(ref, rank):
        zero = jnp.zeros((1, TQ), jnp.int32)
        t0 = jnp.where(count_ge16(ref, zero.astype(jnp.int16)) >= rank, zero, jnp.int32(-HALF_BIAS))

        def bit_step(it, t):
            cand = t | jnp.left_shift(jnp.int32(1), 14 - it)
            return jnp.where(count_ge16(ref, cand.astype(jnp.int16)) >= rank, cand, t)

        return lax.fori_loop(0, 15, bit_step, t0)

    t_hi = kth_largest16(hi_ref, topk)
    above = count_ge16(hi_ref, jnp.minimum(t_hi + 1, HALF_BIAS - 1).astype(jnp.int16))
    above = jnp.where(t_hi >= HALF_BIAS - 1, 0, above)
    t_hi16 = t_hi.astype(jnp.int16)

    def low_tile(j, carry):
        base = pl.multiple_of(j * TK, TK)
        rows = pl.ds(base, TK)
        lo_ref[rows, :] = jnp.where(hi_ref[rows, :] == t_hi16, lo_ref[rows, :], jnp.int16(-HALF_BIAS))
        return carry

    lax.fori_loop(0, nk, low_tile, 0)
    t_lo = kth_largest16(lo_ref, topk - above)
    thr = jnp.left_shift(t_hi, 16) + (t_lo + HALF_BIAS)
    thr = jnp.maximum(thr, jnp.int32(INT_MIN + 1))

    grp = DSA_HEADS // DSA_KV_HEADS
    AK = DSA_ATK
    na = (i * TQ + TQ + AK - 1) // AK
    for g in range(DSA_KV_HEADS):
        qg_ref[g] = jnp.concatenate(
            [dqt_ref[(g * grp + hh) * HEAD_DIM:(g * grp + hh + 1) * HEAD_DIM, :] for hh in range(grp)], axis=1)

    def tile_operands(j, g):
        base = pl.multiple_of(j * AK, AK)
        kt = dk_ref[pl.ds(base, AK), g * HEAD_DIM:(g + 1) * HEAD_DIM]
        vt = vt_ref[g * HEAD_DIM:(g + 1) * HEAD_DIM, pl.ds(base, AK)]
        return kt, vt

    def tile_bias(j, selected):
        base = pl.multiple_of(j * AK, AK)
        bias = jnp.where(sc_ref[pl.ds(base, AK), :] >= thr, selected, NEG_BIG)
        return jnp.concatenate([bias] * grp, axis=1)

    l_ref[...] = jnp.zeros_like(l_ref)
    acc_ref[...] = jnp.zeros_like(acc_ref)
    neg_bound = -bound_ref[0, 0]

    def bounded_tile(j, carry):
        bias = tile_bias(j, neg_bound)
        for g in range(DSA_KV_HEADS):
            kt, vt = tile_operands(j, g)
            p = jnp.exp2(_dot(kt, qg_ref[g]) + bias)
            l_ref[g] += jnp.sum(p, axis=0, keepdims=True)
            acc_ref[g] += _dot(vt, p.astype(BF16))
        return carry

    _loop_by_pairs(na, lambda j: bounded_tile(j, 0))

    @pl.when(jnp.logical_not(jnp.min(l_ref[...]) >= DSA_MIN_DENOM))
    def _():
        m_ref[...] = jnp.full_like(m_ref, NEG_BIG)
        l_ref[...] = jnp.zeros_like(l_ref)
        acc_ref[...] = jnp.zeros_like(acc_ref)

        def online_tile(j, carry):
            bias = tile_bias(j, 0.0)
            for g in range(DSA_KV_HEADS):
                kt, vt = tile_operands(j, g)
                s = _dot(kt, qg_ref[g]) + bias
                m_old = m_ref[g]
                m_new = jnp.maximum(m_old, jnp.max(s, axis=0, keepdims=True))
                p = jnp.exp2(s - m_new)
                alpha = jnp.exp2(m_old - m_new)
                l_ref[g] = alpha * l_ref[g] + jnp.sum(p, axis=0, keepdims=True)
                m_ref[g] = m_new
                acc_ref[g] = alpha * acc_ref[g] + _dot(vt, p.astype(BF16))
            return carry

        lax.fori_loop(0, na, online_tile, 0)

    for g in range(DSA_KV_HEADS):
        o_g = acc_ref[g] * (1.0 / l_ref[g])
        for hh in range(grp):
            h = g * grp + hh
            o_ref[:, h * HEAD_DIM:(h + 1) * HEAD_DIM] = o_g[:, hh * TQ:(hh + 1) * TQ].T.astype(o_ref.dtype)


def _dsa(bound, iqt, iwt, dqt, ik, dk, vt, batch, seq, topk):
    TQ = DSA_TQ
    nq = seq // TQ
    t = batch * seq
    nh = DSA_HEADS * HEAD_DIM
    nkv = DSA_KV_HEADS * HEAD_DIM
    grp = DSA_HEADS // DSA_KV_HEADS
    col = lambda b, i: (0, b * nq + i)
    return pl.pallas_call(
        functools.partial(_dsa_body, topk),
        grid=(batch, nq),
        in_specs=[
            pl.BlockSpec(memory_space=pltpu.SMEM),
            pl.BlockSpec((IDX_HEADS * IDX_DIM, TQ), col),
            pl.BlockSpec((IDX_HEADS, TQ), col),
            pl.BlockSpec((nh, TQ), col),
            pl.BlockSpec((seq, IDX_DIM), lambda b, i: (b, 0), pipeline_mode=pl.Buffered(1)),
            pl.BlockSpec((seq, nkv), lambda b, i: (b, 0), pipeline_mode=pl.Buffered(1)),
            pl.BlockSpec((nkv, seq), lambda b, i: (0, b), pipeline_mode=pl.Buffered(1)),
        ],
        out_specs=pl.BlockSpec((TQ, nh), lambda b, i: (b * nq + i, 0)),
        out_shape=jax.ShapeDtypeStruct((t, nh), BF16),
        scratch_shapes=[
            pltpu.VMEM((seq, TQ), jnp.int32),
            pltpu.VMEM((seq, TQ), jnp.int16),
            pltpu.VMEM((seq, TQ), jnp.int16),
            pltpu.VMEM((DSA_KV_HEADS, HEAD_DIM, grp * TQ), BF16),
            pltpu.VMEM((DSA_KV_HEADS, 1, grp * TQ), F32),
            pltpu.VMEM((DSA_KV_HEADS, 1, grp * TQ), F32),
            pltpu.VMEM((DSA_KV_HEADS, HEAD_DIM, grp * TQ), F32),
        ],
        compiler_params=_params("parallel", "arbitrary"),
        name="dsa",
    )(bound, iqt, iwt, dqt, ik, dk, vt)


def _merge_body(og_ref, od_ref, wg_ref, wd_ref, gg_ref, gd_ref, bgg_ref, bgd_ref, o_ref):
    yg = _dot(og_ref[...], wg_ref[...])
    yd = _dot(od_ref[...], wd_ref[...])
    sg = 1.0 / (1.0 + jnp.exp(-(gg_ref[...] + bgg_ref[...])))
    sd = 1.0 / (1.0 + jnp.exp(-(gd_ref[...] + bgd_ref[...])))
    o_ref[...] = (sg * yg + sd * yd).astype(o_ref.dtype)


def _merge(o_gla, o_dsa, wg, wd, p_gate, b_gate, tm=1024, tn=512):
    t, dg = o_gla.shape
    dd = o_dsa.shape[1]
    d = wg.shape[1]
    nb = d // tn
    return pl.pallas_call(
        _merge_body,
        grid=(t // tm, nb),
        in_specs=[
            pl.BlockSpec((tm, dg), lambda i, j: (i, 0)),
            pl.BlockSpec((tm, dd), lambda i, j: (i, 0)),
            pl.BlockSpec((dg, tn), lambda i, j: (0, j)),
            pl.BlockSpec((dd, tn), lambda i, j: (0, j)),
            pl.BlockSpec((tm, tn), lambda i, j: (i, j)),
            pl.BlockSpec((tm, tn), lambda i, j: (i, nb + j)),
            pl.BlockSpec((1, tn), lambda i, j: (0, j)),
            pl.BlockSpec((1, tn), lambda i, j: (0, nb + j)),
        ],
        out_specs=pl.BlockSpec((tm, tn), lambda i, j: (i, j)),
        out_shape=jax.ShapeDtypeStruct((t, d), BF16),
        compiler_params=_params("parallel", "parallel"),
        name="merge",
    )(o_gla, o_dsa, wg, wd, p_gate, p_gate, b_gate, b_gate)


def _outproj_body(x_ref, m_ref, w_ref, o_ref):
    o_ref[...] = x_ref[...] + _dot(m_ref[...], w_ref[...])


def _outproj(x2, m, w_out, tm=1024, tn=1024):
    t, d = x2.shape
    tn = min(tn, d)
    return pl.pallas_call(
        _outproj_body,
        grid=(t // tm, d // tn),
        in_specs=[
            pl.BlockSpec((tm, tn), lambda i, j: (i, j)),
            pl.BlockSpec((tm, d), lambda i, j: (i, 0)),
            pl.BlockSpec((d, tn), lambda i, j: (0, j)),
        ],
        out_specs=pl.BlockSpec((tm, tn), lambda i, j: (i, j)),
        out_shape=jax.ShapeDtypeStruct((t, d), F32),
        compiler_params=_params("parallel", "parallel"),
        name="out_proj",
    )(x2, m, w_out)


def _mlp_body(x_ref, g_ref, w1_ref, w2_ref, o_ref, h_scr):
    @pl.when(pl.program_id(1) == 0)
    def _():
        x = x_ref[...]
        h_scr[...] = (x * lax.rsqrt(jnp.mean(x * x, axis=-1, keepdims=True) + NORM_EPS) * g_ref[...]).astype(BF16)
        o_ref[...] = x

    u = jnp.maximum(_dot(h_scr[...], w1_ref[...]), 0.0)
    o_ref[...] += _dot((u * u).astype(BF16), w2_ref[...])


def _mlp(x1, g, w1, w2, tm=512, tf=1024):
    t, d = x1.shape
    ff = w1.shape[1]
    return pl.pallas_call(
        _mlp_body,
        grid=(t // tm, ff // tf),
        in_specs=[
            pl.BlockSpec((tm, d), lambda i, f: (i, 0)),
            pl.BlockSpec((1, d), lambda i, f: (0, 0)),
            pl.BlockSpec((d, tf), lambda i, f: (0, f)),
            pl.BlockSpec((tf, d), lambda i, f: (f, 0)),
        ],
        out_specs=pl.BlockSpec((tm, d), lambda i, f: (i, 0)),
        out_shape=jax.ShapeDtypeStruct((t, d), F32),
        scratch_shapes=[pltpu.VMEM((tm, d), BF16)],
        compiler_params=_params("parallel", "arbitrary"),
        name="mlp",
    )(x1, g.reshape(1, d), w1, w2)


def _rope_tables(seq, q_g, k_g, ik_g):
    pos = jnp.arange(seq, dtype=jnp.int32).astype(F32)

    def cs(rot):
        half = rot // 2
        inv = ROPE_THETA ** (-jnp.arange(half, dtype=F32) * 2.0 / rot)
        ang = pos[:, None] * inv[None, :]
        return jnp.cos(ang), jnp.sin(ang)

    c, s = cs(HEAD_DIM)
    hh = HEAD_DIM // 2
    ka = jnp.concatenate([c, c], axis=1) * k_g[None, :]
    kb = jnp.concatenate([-s * k_g[None, hh:], s * k_g[None, :hh]], axis=1)
    qs = (HEAD_DIM ** -0.5) * LOG2E
    qa = (jnp.concatenate([c, c], axis=1) * q_g[None, :] * qs).T
    qb = (jnp.concatenate([-s * q_g[None, hh:], s * q_g[None, :hh]], axis=1) * qs).T

    ci, si = cs(IDX_ROPE_DIM)
    h2 = IDX_ROPE_DIM // 2
    one = jnp.ones((seq, IDX_DIM - IDX_ROPE_DIM), F32)
    zero = jnp.zeros((seq, IDX_DIM - IDX_ROPE_DIM), F32)
    zh = jnp.zeros((seq, h2), F32)
    ia = jnp.concatenate([ci, ci, one], axis=1) * ik_g[None, :]
    ib1 = jnp.concatenate([-si * ik_g[None, h2:2 * h2], zh, zero], axis=1)
    ib2 = jnp.concatenate([zh, si * ik_g[None, :h2], zero], axis=1)
    iqs = IDX_DIM ** -0.5
    iqa = (jnp.concatenate([ci, ci, one], axis=1) * iqs).T
    iqb = (jnp.concatenate([-si, si, zero], axis=1) * iqs).T
    return (ka, kb, ia, ib1, ib2), (qa, qb), (iqa, iqb)


def kernel(x, norm1_g, w_in, gla_wg2, gla_bg, gla_norm_g, w_proj_gla, q_norm_g, k_norm_g, idx_k_norm_g,
           w_proj_dsa, b_gate, w_out, norm2_g, w_ff1, w_ff2):
    batch, seq, d = x.shape
    depth = w_in.shape[0]
    topk = min(TOPK_MAX, seq // 4)
    nqk = GLA_HEADS * GLA_DK
    nv = GLA_HEADS * GLA_DV
    ndq = DSA_HEADS * HEAD_DIM
    nkv = DSA_KV_HEADS * HEAD_DIM
    niq = IDX_HEADS * IDX_DIM
    splits = (nqk, nqk, nv, nv, GLA_GATE_RANK, ndq, nkv, nkv, niq, IDX_DIM, IDX_HEADS, 2 * d)
    offs = [0]
    for n in splits:
        offs.append(offs[-1] + n)
    o_gq, _, _, _, o_glr, o_dq, o_dk, o_dv, o_iq, o_ik, o_iw, o_gate, _ = offs
    assert (2 * nqk + 2 * nv) % 1024 == 0 and seq % 512 == 0 and d % 512 == 0

    x2 = x.reshape(batch * seq, d)
    for l in range(depth):
        w = w_in[l]
        w_gla = w[:, o_gq:o_glr].astype(BF16)
        w_gate = w[:, o_gate:].astype(BF16)
        glr_pad = jnp.zeros((d, LANES - GLA_GATE_RANK), F32)
        w_k = jnp.concatenate([w[:, o_dk:o_dv], w[:, o_ik:o_iw], w[:, o_glr:o_dq], glr_pad], axis=1).astype(BF16)
        wt_dq = w[:, o_dq:o_dk].T.astype(BF16)
        wt_iq = w[:, o_iq:o_ik].T.astype(BF16)
        wt_v = w[:, o_dv:o_iq].T.astype(BF16)
        wt_iw = w[:, o_iw:o_gate].T.astype(BF16)
        wg2p = jnp.concatenate(
            [gla_wg2[l], jnp.zeros((LANES - GLA_GATE_RANK, nqk), F32)], axis=0).astype(BF16)
        k_tabs, q_tabs, iq_tabs = _rope_tables(seq, q_norm_g[l], k_norm_g[l], idx_k_norm_g[l])

        h, ht = _rmsnorm_both(x2, norm1_g[l])
        p_gla = _mm(h, w_gla, F32, 1024, 1024, "proj_gla")
        p_gate = _mm(h, w_gate, F32, 1024, 1024, "proj_gate")
        dk, ik, glr = _proj_k(h, w_k, k_tabs, seq)
        dqt = _proj_t("dq", wt_dq, ht, q_tabs, seq, BF16, 1024)
        iqt = _proj_t("iq", wt_iq, ht, iq_tabs, seq, BF16, 1024)
        vt = _proj_t("v", wt_v, ht, (), seq, BF16, nkv)
        iwt = _proj_t("iw", wt_iw, ht, (), seq, F32, IDX_HEADS)

        o_gla = _gla(p_gla, glr, wg2p, gla_bg[l].reshape(1, nqk), gla_norm_g[l].reshape(1, GLA_DV), batch, seq)
        bound = (1.02 * HEAD_DIM * (HEAD_DIM ** -0.5) * LOG2E
                 * jnp.max(jnp.abs(q_norm_g[l])) * jnp.max(jnp.abs(k_norm_g[l]))).reshape(1, 1).astype(F32)
        o_dsa = _dsa(bound, iqt, iwt, dqt, ik, dk, vt, batch, seq, topk)

        m = _merge(o_gla, o_dsa, w_proj_gla[l].astype(BF16), w_proj_dsa[l].astype(BF16), p_gate,
                   b_gate[l].reshape(1, 2 * d))
        x1 = _outproj(x2, m, w_out[l].astype(BF16))
        x2 = _mlp(x1, norm2_g[l], w_ff1[l].astype(BF16), w_ff2[l].astype(BF16))
    return x2.reshape(batch, seq, d)
```

```python
import functools
import math

import jax
import jax.numpy as jnp
from jax import lax
from jax.experimental import pallas as pl
from jax.experimental.pallas import tpu as pltpu

BF16 = jnp.bfloat16
F32 = jnp.float32

NORM_EPS = 1e-6
ROPE_THETA = 10000.0
GLA_HEADS = 4
GLA_DK = 256
GLA_DV = 512
GLA_GATE_RANK = 16
GLA_GATE_NORMALIZER = 16.0
DSA_HEADS = 16
DSA_KV_HEADS = 4
HEAD_DIM = 128
IDX_HEADS = 16
IDX_DIM = 128
IDX_ROPE_DIM = 64
TOPK_MAX = 256

LANES = 128
VMEM_LIMIT = 56 * 1024 * 1024
GLA_CHUNK = 128
GLA_SUB = 16
GLA_MAX_SPLIT_LOG2 = 100.0
DSA_TQ = 256
DSA_TK = 512
DSA_ATK = 512
DSA_COUNT_ROWS = 64
DSA_MIN_DENOM = 2.0 ** -80
INT_MIN = -(2 ** 31)
HALF_BIAS = 2 ** 15
NEG_BIG = -1e30
LOG2E = math.log2(math.e)


def _loop_by_pairs(n, body):
    def pair(jj, carry):
        body(2 * jj)
        body(2 * jj + 1)
        return carry

    lax.fori_loop(0, n // 2, pair, 0)

    @pl.when(n % 2 == 1)
    def _():
        body(n - 1)


def _params(*sem):
    return pltpu.CompilerParams(dimension_semantics=sem, vmem_limit_bytes=VMEM_LIMIT)


def _dot(a, b):
    return jnp.dot(a, b, preferred_element_type=F32)


def _repack_body(offs, w_ref, gla_ref, gate_ref, k_ref, tdq_ref, tiq_ref, tv_ref):
    o_gq, o_glr, o_dq, o_dk, o_dv, o_iq, o_ik, o_iw, o_gate, o_end = offs
    gla_ref[...] = w_ref[:, o_gq:o_glr].astype(BF16)
    gate_ref[...] = w_ref[:, o_gate:o_end].astype(BF16)
    tdq_ref[...] = w_ref[:, o_dq:o_dk].T.astype(BF16)
    tiq_ref[...] = w_ref[:, o_iq:o_ik].T.astype(BF16)
    tv_ref[...] = w_ref[:, o_dv:o_iq].T.astype(BF16)
    lane = lax.broadcasted_iota(jnp.int32, (w_ref.shape[0], LANES), 1)

    def narrow(start, width):
        return jnp.where(lane < width, w_ref[:, start:start + LANES], 0.0).astype(BF16)

    nkv = o_dv - o_dk
    k_ref[:, :nkv] = w_ref[:, o_dk:o_dv].astype(BF16)
    k_ref[:, nkv:nkv + IDX_DIM] = w_ref[:, o_ik:o_iw].astype(BF16)
    k_ref[:, nkv + IDX_DIM:nkv + IDX_DIM + LANES] = narrow(o_glr, GLA_GATE_RANK)
    k_ref[:, nkv + IDX_DIM + LANES:] = narrow(o_iw, IDX_HEADS)


def _repack_w_in(w, offs, tr=128):
    d, ncol = w.shape
    o_gq, o_glr, o_dq, o_dk, o_dv, o_iq, o_ik, o_iw, o_gate, o_end = offs
    n_gla, n_gate, n_dq, n_iq, n_v = o_glr - o_gq, o_end - o_gate, o_dk - o_dq, o_ik - o_iq, o_iq - o_dv
    n_k = (o_dv - o_dk) + IDX_DIM + 2 * LANES
    row = lambda n: pl.BlockSpec((tr, n), lambda i: (i, 0))
    col = lambda n: pl.BlockSpec((n, tr), lambda i: (0, i))
    sds = jax.ShapeDtypeStruct
    return pl.pallas_call(
        functools.partial(_repack_body, offs),
        grid=(d // tr,),
        in_specs=[row(ncol)],
        out_specs=[row(n_gla), row(n_gate), row(n_k), col(n_dq), col(n_iq), col(n_v)],
        out_shape=[sds((d, n_gla), BF16), sds((d, n_gate), BF16), sds((d, n_k), BF16),
                   sds((n_dq, d), BF16), sds((n_iq, d), BF16), sds((n_v, d), BF16)],
        compiler_params=_params("parallel"),
        name="repack_w_in",
    )(w)


def _rms_body(x_ref, g_ref, h_ref, ht_ref):
    x = x_ref[...]
    y = x * lax.rsqrt(jnp.mean(x * x, axis=-1, keepdims=True) + NORM_EPS) * g_ref[...]
    h_ref[...] = y.astype(BF16)
    ht_ref[...] = y.T.astype(BF16)


def _rmsnorm_both(x2, g, tm=256):
    t, d = x2.shape
    return pl.pallas_call(
        _rms_body,
        grid=(t // tm,),
        in_specs=[pl.BlockSpec((tm, d), lambda i: (i, 0)), pl.BlockSpec((1, d), lambda i: (0, 0))],
        out_specs=[pl.BlockSpec((tm, d), lambda i: (i, 0)), pl.BlockSpec((d, tm), lambda i: (0, i))],
        out_shape=[jax.ShapeDtypeStruct((t, d), BF16), jax.ShapeDtypeStruct((d, t), BF16)],
        compiler_params=_params("parallel"),
        name="rmsnorm_in",
    )(x2, g.reshape(1, d))


def _mm_body(a_ref, w_ref, o_ref):
    o_ref[...] = _dot(a_ref[...], w_ref[...]).astype(o_ref.dtype)


def _mm(a, w, out_dtype, tm, tn, name):
    m, k = a.shape
    n = w.shape[1]
    return pl.pallas_call(
        _mm_body,
        grid=(m // tm, n // tn),
        in_specs=[pl.BlockSpec((tm, k), lambda i, j: (i, 0)), pl.BlockSpec((k, tn), lambda i, j: (0, j))],
        out_specs=pl.BlockSpec((tm, tn), lambda i, j: (i, j)),
        out_shape=jax.ShapeDtypeStruct((m, n), out_dtype),
        compiler_params=_params("parallel", "parallel"),
        name=name,
    )(a, w)


def _projk_body(h_ref, w_ref, ka_ref, kb_ref, ia_ref, ib1_ref, ib2_ref, dk_ref, ik_ref, glr_ref, iw_ref):
    p = _dot(h_ref[...], w_ref[...])
    nkv = DSA_KV_HEADS
    for j in range(nkv):
        pj = p[:, j * HEAD_DIM:(j + 1) * HEAD_DIM]
        n = pj * lax.rsqrt(jnp.mean(pj * pj, axis=-1, keepdims=True) + NORM_EPS)
        out = n * ka_ref[...] + pltpu.roll(n, HEAD_DIM // 2, 1) * kb_ref[...]
        dk_ref[:, j * HEAD_DIM:(j + 1) * HEAD_DIM] = out.astype(BF16)
    c0 = nkv * HEAD_DIM
    pi = p[:, c0:c0 + IDX_DIM]
    n = pi * lax.rsqrt(jnp.mean(pi * pi, axis=-1, keepdims=True) + NORM_EPS)
    half = IDX_ROPE_DIM // 2
    out = n * ia_ref[...] + pltpu.roll(n, IDX_DIM - half, 1) * ib1_ref[...] + pltpu.roll(n, half, 1) * ib2_ref[...]
    ik_ref[...] = out.astype(BF16)
    c1 = c0 + IDX_DIM
    glr_ref[...] = p[:, c1:c1 + LANES]
    iw_ref[...] = p[:, c1 + LANES:] * (IDX_HEADS ** -0.5)


def _proj_k(h, w_k, tabs, seq, tm=512):
    t, d = h.shape
    nblk = seq // tm
    ncol = w_k.shape[1]
    tab_spec = pl.BlockSpec((tm, LANES), lambda i: (i % nblk, 0))
    return pl.pallas_call(
        _projk_body,
        grid=(t // tm,),
        in_specs=[pl.BlockSpec((tm, d), lambda i: (i, 0)), pl.BlockSpec((d, ncol), lambda i: (0, 0))] + [tab_spec] * 5,
        out_specs=[pl.BlockSpec((tm, DSA_KV_HEADS * HEAD_DIM), lambda i: (i, 0)),
                   pl.BlockSpec((tm, IDX_DIM), lambda i: (i, 0)),
                   pl.BlockSpec((tm, LANES), lambda i: (i, 0)),
                   pl.BlockSpec((tm, LANES), lambda i: (i, 0))],
        out_shape=[jax.ShapeDtypeStruct((t, DSA_KV_HEADS * HEAD_DIM), BF16),
                   jax.ShapeDtypeStruct((t, IDX_DIM), BF16),
                   jax.ShapeDtypeStruct((t, LANES), F32),
                   jax.ShapeDtypeStruct((t, LANES), F32)],
        compiler_params=_params("parallel"),
        name="proj_k",
    )(h, w_k, *tabs)


def _projt_body(kind, w_ref, ht_ref, *rest):
    p = _dot(w_ref[...], ht_ref[...])
    o_ref = rest[-1]
    rows = p.shape[0]
    if kind == "v":
        o_ref[...] = p.astype(o_ref.dtype)
        return
    ta_ref, tb_ref = rest[0], rest[1]
    for j in range(rows // HEAD_DIM):
        pj = p[j * HEAD_DIM:(j + 1) * HEAD_DIM]
        if kind == "dq":
            n = pj * lax.rsqrt(jnp.mean(pj * pj, axis=0, keepdims=True) + NORM_EPS)
            sw = jnp.concatenate([n[HEAD_DIM // 2:], n[:HEAD_DIM // 2]], axis=0)
        else:
            n = pj
            half = IDX_ROPE_DIM // 2
            sw = jnp.concatenate([n[half:2 * half], n[:half], n[2 * half:]], axis=0)
        o_ref[j * HEAD_DIM:(j + 1) * HEAD_DIM, :] = (n * ta_ref[...] + sw * tb_ref[...]).astype(o_ref.dtype)


def _proj_t(kind, w_t, ht, tabs, seq, out_dtype, tc, tt=512):
    c, d = w_t.shape
    t = ht.shape[1]
    nblk = seq // tt
    tab_spec = pl.BlockSpec((HEAD_DIM, tt), lambda i, j: (0, j % nblk))
    return pl.pallas_call(
        functools.partial(_projt_body, kind),
        grid=(c // tc, t // tt),
        in_specs=[pl.BlockSpec((tc, d), lambda i, j: (i, 0)), pl.BlockSpec((d, tt), lambda i, j: (0, j))]
        + [tab_spec] * len(tabs),
        out_specs=pl.BlockSpec((tc, tt), lambda i, j: (i, j)),
        out_shape=jax.ShapeDtypeStruct((c, t), out_dtype),
        compiler_params=_params("parallel", "parallel"),
        name="proj_t_" + kind,
    )(w_t, ht, *tabs)


def _gla_scores_explicit(q, k, b, bt):
    C, SB = GLA_CHUNK, GLA_SUB
    kt = k.T
    lane = lax.broadcasted_iota(jnp.int32, (SB, C), 1)
    rowi = lax.broadcasted_iota(jnp.int32, (SB, 1), 0)
    blocks = []
    for blk in range(C // SB):
        s = blk * SB
        q_i = q[s:s + SB]
        b_i = b[s:s + SB]
        a = jnp.zeros((SB, C), F32)
        for j in range(SB):
            r = s + j
            dec = jnp.where(rowi >= j, jnp.exp2(b_i - b[r:r + 1, :]), 0.0)
            col = jnp.sum(q_i * k[r:r + 1, :] * dec, axis=-1, keepdims=True)
            a = jnp.where(lane == r, col, a)
        if blk > 0:
            q_s = q_i * jnp.exp2(b_i - b[s - 1:s, :])
            k_s = kt * jnp.exp2(jnp.minimum(bt[:, s - 1:s] - bt, 0.0))
            a = jnp.where(lane < s, _dot(q_s.astype(BF16), k_s.astype(BF16)), a)
        blocks.append(a)
    return jnp.concatenate(blocks, axis=0)


def _gla_scores_split(q, k, b):
    C, SB = GLA_CHUNK, GLA_SUB
    nt = (((1,), (1,)), ((), ()))
    ri = lax.broadcasted_iota(jnp.int32, (C, C), 0)
    ci = lax.broadcasted_iota(jnp.int32, (C, C), 1)
    zeros = lambda n: jnp.zeros((n, GLA_DK), F32)
    a = jnp.zeros((C, C), F32)
    span = C // 2
    while span >= SB:
        q_parts, k_parts = [], []
        for lo in range(0, C, 2 * span):
            mid, hi = lo + span, lo + 2 * span
            ref = b[mid - 1:mid, :]
            k_parts += [k[lo:mid] * jnp.exp2(ref - b[lo:mid]), zeros(span)]
            q_parts += [zeros(span), q[mid:hi] * jnp.exp2(b[mid:hi] - ref)]
        s_l = lax.dot_general(jnp.concatenate(q_parts, axis=0).astype(BF16),
                              jnp.concatenate(k_parts, axis=0).astype(BF16), nt, preferred_element_type=F32)
        shift = (2 * span).bit_length() - 1
        a = a + jnp.where(jnp.right_shift(ri, shift) == jnp.right_shift(ci, shift), s_l, 0.0)
        span //= 2
    q_parts, k_parts = [], []
    for lo in range(0, C, SB):
        gap = b[lo:lo + SB] - b[lo - 1:lo, :] if lo > 0 else b[lo:lo + SB]
        q_parts.append(q[lo:lo + SB] * jnp.exp2(gap))
        k_parts.append(k[lo:lo + SB] * jnp.exp2(jnp.minimum(-gap, GLA_MAX_SPLIT_LOG2)))
    s_d = lax.dot_general(jnp.concatenate(q_parts, axis=0).astype(BF16),
                          jnp.concatenate(k_parts, axis=0).astype(BF16), nt, preferred_element_type=F32)
    shift = SB.bit_length() - 1
    same = jnp.right_shift(ri, shift) == jnp.right_shift(ci, shift)
    return a + jnp.where(same, jnp.where(ci <= ri, s_d, 0.0), 0.0)


def _gla_head(h, explicit, q_ref, k_ref, v_ref, gr_ref, ng_ref, o_ref, st_ref, b_ref):
    C = GLA_CHUNK
    qc = slice(h * GLA_DK, (h + 1) * GLA_DK)
    vc = slice(h * GLA_DV, (h + 1) * GLA_DV)
    q = q_ref[:, qc] * (GLA_DK ** -0.5)
    k = k_ref[:, qc]
    vb = v_ref[:, vc].astype(BF16)
    b = b_ref[:, qc]
    bt = b.T

    o = _dot((q * jnp.exp2(b)).astype(BF16), st_ref[h].astype(BF16))
    a = _gla_scores_explicit(q, k, b, bt) if explicit else _gla_scores_split(q, k, b)
    o = o + _dot(a.astype(BF16), vb)

    k_dec = k * jnp.exp2(b[C - 1:C, :] - b)
    st_ref[h] = st_ref[h] * jnp.exp2(bt[:, C - 1:C]) + _dot(k_dec.T.astype(BF16), vb)

    on = o * lax.rsqrt(jnp.mean(o * o, axis=-1, keepdims=True) + NORM_EPS) * ng_ref[...]
    gr = gr_ref[:, vc]
    o_ref[:, vc] = (on * gr * (1.0 / (1.0 + jnp.exp(-gr)))).astype(o_ref.dtype)


def _decay_body(glr_ref, wg2_ref, bg_ref, b_ref, drop_ref):
    C, SB = GLA_CHUNK, GLA_SUB
    z = _dot(glr_ref[...].astype(BF16), wg2_ref[...]) + bg_ref[...]
    g = (jnp.minimum(z, 0.0) - jnp.log1p(jnp.exp(-jnp.abs(z)))) * (LOG2E / GLA_GATE_NORMALIZER)
    ri = lax.broadcasted_iota(jnp.int32, (C, C), 0)
    ci = lax.broadcasted_iota(jnp.int32, (C, C), 1)
    tri = jnp.where(ci <= ri, 1.0, 0.0).astype(BF16)
    g1 = g.astype(BF16)
    r1 = g - g1.astype(F32)
    g2 = r1.astype(BF16)
    g3 = (r1 - g2.astype(F32)).astype(BF16)
    drop_ref[...] = jnp.zeros_like(drop_ref)
    for ch in range(glr_ref.shape[0] // C):
        rows = slice(ch * C, (ch + 1) * C)
        b = _dot(tri, g1[rows]) + _dot(tri, g2[rows]) + _dot(tri, g3[rows])
        b_ref[rows, :] = b
        drop = -b[SB - 1:SB, :]
        for blk in range(1, C // SB):
            s = blk * SB
            drop = jnp.maximum(drop, b[s - 1:s, :] - b[s + SB - 1:s + SB, :])
        drop_ref[0, ch:ch + 1, :] = jnp.broadcast_to(jnp.max(drop, axis=-1, keepdims=True), (1, LANES))


def _gla_decay(glr, wg2p, bg, tm=512):
    t = glr.shape[0]
    nqk = wg2p.shape[1]
    assert tm // GLA_CHUNK <= 8
    return pl.pallas_call(
        _decay_body,
        grid=(t // tm,),
        in_specs=[
            pl.BlockSpec((tm, LANES), lambda i: (i, 0)),
            pl.BlockSpec((LANES, nqk), lambda i: (0, 0)),
            pl.BlockSpec((1, nqk), lambda i: (0, 0)),
        ],
        out_specs=[pl.BlockSpec((tm, nqk), lambda i: (i, 0)), pl.BlockSpec((1, 8, LANES), lambda i: (i, 0, 0))],
        out_shape=[jax.ShapeDtypeStruct((t, nqk), F32), jax.ShapeDtypeStruct((t // tm, 8, LANES), F32)],
        compiler_params=_params("parallel"),
        name="gla_decay",
    )(glr, wg2p, bg)


def _gla_body(mild_ref, q_ref, k_ref, v_ref, gr_ref, b_ref, ng_ref, o_ref, st_ref):
    @pl.when(pl.program_id(1) == 0)
    def _():
        st_ref[...] = jnp.zeros_like(st_ref)

    refs = (q_ref, k_ref, v_ref, gr_ref, ng_ref, o_ref, st_ref, b_ref)
    mild = mild_ref[pl.program_id(0) * pl.num_programs(1) + pl.program_id(1)] == 1

    @pl.when(mild)
    def _():
        for h in range(GLA_HEADS):
            _gla_head(h, False, *refs)

    @pl.when(jnp.logical_not(mild))
    def _():
        for h in range(GLA_HEADS):
            _gla_head(h, True, *refs)


def _gla(p_gla, b_cum, mild, ng, batch, seq):
    C = GLA_CHUNK
    nc = seq // C
    t = batch * seq
    nqk = GLA_HEADS * GLA_DK
    nv = GLA_HEADS * GLA_DV
    vblk = 2 * nqk // nv
    row = lambda b, c: b * nc + c
    return pl.pallas_call(
        _gla_body,
        grid_spec=pltpu.PrefetchScalarGridSpec(
            num_scalar_prefetch=1,
            grid=(batch, nc),
            in_specs=[
                pl.BlockSpec((C, nqk), lambda b, c, m: (row(b, c), 0)),
                pl.BlockSpec((C, nqk), lambda b, c, m: (row(b, c), 1)),
                pl.BlockSpec((C, nv), lambda b, c, m: (row(b, c), vblk)),
                pl.BlockSpec((C, nv), lambda b, c, m: (row(b, c), vblk + 1)),
                pl.BlockSpec((C, nqk), lambda b, c, m: (row(b, c), 0)),
                pl.BlockSpec((1, GLA_DV), lambda b, c, m: (0, 0)),
            ],
            out_specs=pl.BlockSpec((C, nv), lambda b, c, m: (row(b, c), 0)),
            scratch_shapes=[pltpu.VMEM((GLA_HEADS, GLA_DK, GLA_DV), F32)],
        ),
        out_shape=jax.ShapeDtypeStruct((t, nv), BF16),
        compiler_params=_params("parallel", "arbitrary"),
        name="gla",
    )(mild, p_gla, p_gla, p_gla, p_gla, b_cum, ng)


def _dsa_body(topk, bound_ref, iqt_ref, iw_ref, dqt_ref, ik_ref, dk_ref, vt_ref, o_ref,
              sc_ref, hi_ref, lo_ref, iwt_ref, qg_ref, m_ref, l_ref, acc_ref):
    TQ, TK = DSA_TQ, DSA_TK
    i = pl.program_id(1)
    nk = (i * TQ + TQ + TK - 1) // TK
    q_pos = i * TQ + lax.broadcasted_iota(jnp.int32, (TK, TQ), 1)
    k_off = lax.broadcasted_iota(jnp.int32, (TK, TQ), 0)
    iwt_ref[...] = iw_ref[...].T

    def score_tile(j, carry):
        base = pl.multiple_of(j * TK, TK)
        ikt = ik_ref[pl.ds(base, TK), :]
        acc = jnp.zeros((TK, TQ), F32)
        for h in range(IDX_HEADS):
            r = _dot(ikt, iqt_ref[h * IDX_DIM:(h + 1) * IDX_DIM, :])
            acc = acc + jnp.maximum(r, 0.0) * iwt_ref[h:h + 1, :]
        bits = pltpu.bitcast(acc, jnp.int32)
        key = jnp.where(bits < 0, bits ^ jnp.int32(0x7FFFFFFF), bits)
        key = jnp.where(base + k_off <= q_pos, key, jnp.int32(INT_MIN))
        sc_ref[pl.ds(base, TK), :] = key
        hi_ref[pl.ds(base, TK), :] = jnp.right_shift(key, 16).astype(jnp.int16)
        lo_ref[pl.ds(base, TK), :] = ((key & jnp.int32(0xFFFF)) - jnp.int32(HALF_BIAS)).astype(jnp.int16)
        return carry

    _loop_by_pairs(nk, lambda j: score_tile(j, 0))

    CR = DSA_COUNT_ROWS

    def count_ge16(ref, cand16):
        def body(j, cnt):
            base = pl.multiple_of(j * TK, TK)
            hit = jnp.where(ref[pl.ds(base, TK), :] >= cand16, jnp.int16(1), jnp.int16(0))
            for r in range(TK // CR):
                cnt = cnt + hit[r * CR:(r + 1) * CR]
            return cnt
        cnt = lax.fori_loop(0, nk, body, jnp.zeros((CR, TQ), jnp.int16))
        return jnp.sum(cnt.astype(jnp.int32), axis=0, keepdims=True)

    def kth_largest16(ref, rank):
        zero = jnp.zeros((1, TQ), jnp.int32)
        t0 = jnp.where(count_ge16(ref, zero.astype(jnp.int16)) >= rank, zero, jnp.int32(-HALF_BIAS))

        def bit_step(it, t):
            cand = t | jnp.left_shift(jnp.int32(1), 14 - it)
            return jnp.where(count_ge16(ref, cand.astype(jnp.int16)) >= rank, cand, t)

        return lax.fori_loop(0, 15, bit_step, t0)

    t_hi = kth_largest16(hi_ref, topk)
    above = count_ge16(hi_ref, jnp.minimum(t_hi + 1, HALF_BIAS - 1).astype(jnp.int16))
    above = jnp.where(t_hi >= HALF_BIAS - 1, 0, above)
    t_hi16 = t_hi.astype(jnp.int16)

    def low_tile(j, carry):
        base = pl.multiple_of(j * TK, TK)
        rows = pl.ds(base, TK)
        lo_ref[rows, :] = jnp.where(hi_ref[rows, :] == t_hi16, lo_ref[rows, :], jnp.int16(-HALF_BIAS))
        return carry

    lax.fori_loop(0, nk, low_tile, 0)
    t_lo = kth_largest16(lo_ref, topk - above)
    thr = jnp.left_shift(t_hi, 16) + (t_lo + HALF_BIAS)
    thr = jnp.maximum(thr, jnp.int32(INT_MIN + 1))

    grp = DSA_HEADS // DSA_KV_HEADS
    AK = DSA_ATK
    na = (i * TQ + TQ + AK - 1) // AK
    for g in range(DSA_KV_HEADS):
        qg_ref[g] = jnp.concatenate(
            [dqt_ref[(g * grp + hh) * HEAD_DIM:(g * grp + hh + 1) * HEAD_DIM, :] for hh in range(grp)], axis=1)

    def tile_operands(j, g):
        base = pl.multiple_of(j * AK, AK)
        kt = dk_ref[pl.ds(base, AK), g * HEAD_DIM:(g + 1) * HEAD_DIM]
        vt = vt_ref[g * HEAD_DIM:(g + 1) * HEAD_DIM, pl.ds(base, AK)]
        return kt, vt

    def tile_bias(j, selected):
        base = pl.multiple_of(j * AK, AK)
        bias = jnp.where(sc_ref[pl.ds(base, AK), :] >= thr, selected, NEG_BIG)
        return jnp.concatenate([bias] * grp, axis=1)

    l_ref[...] = jnp.zeros_like(l_ref)
    acc_ref[...] = jnp.zeros_like(acc_ref)
    neg_bound = -bound_ref[0, 0]

    def bounded_tile(j, carry):
        bias = tile_bias(j, neg_bound)
        for g in range(DSA_KV_HEADS):
            kt, vt = tile_operands(j, g)
            p = jnp.exp2(_dot(kt, qg_ref[g]) + bias)
            l_ref[g] += jnp.sum(p, axis=0, keepdims=True)
            acc_ref[g] += _dot(vt, p.astype(BF16))
        return carry

    _loop_by_pairs(na, lambda j: bounded_tile(j, 0))

    @pl.when(jnp.logical_not(jnp.min(l_ref[...]) >= DSA_MIN_DENOM))
    def _():
        m_ref[...] = jnp.full_like(m_ref, NEG_BIG)
        l_ref[...] = jnp.zeros_like(l_ref)
        acc_ref[...] = jnp.zeros_like(acc_ref)

        def online_tile(j, carry):
            bias = tile_bias(j, 0.0)
            for g in range(DSA_KV_HEADS):
                kt, vt = tile_operands(j, g)
                s = _dot(kt, qg_ref[g]) + bias
                m_old = m_ref[g]
                m_new = jnp.maximum(m_old, jnp.max(s, axis=0, keepdims=True))
                p = jnp.exp2(s - m_new)
                alpha = jnp.exp2(m_old - m_new)
                l_ref[g] = alpha * l_ref[g] + jnp.sum(p, axis=0, keepdims=True)
                m_ref[g] = m_new
                acc_ref[g] = alpha * acc_ref[g] + _dot(vt, p.astype(BF16))
            return carry

        lax.fori_loop(0, na, online_tile, 0)

    for g in range(DSA_KV_HEADS):
        o_g = acc_ref[g] * (1.0 / l_ref[g])
        for hh in range(grp):
            h = g * grp + hh
            o_ref[:, h * HEAD_DIM:(h + 1) * HEAD_DIM] = o_g[:, hh * TQ:(hh + 1) * TQ].T.astype(o_ref.dtype)


def _dsa(bound, iqt, iw, dqt, ik, dk, vt, batch, seq, topk):
    TQ = DSA_TQ
    nq = seq // TQ
    t = batch * seq
    nh = DSA_HEADS * HEAD_DIM
    nkv = DSA_KV_HEADS * HEAD_DIM
    grp = DSA_HEADS // DSA_KV_HEADS
    col = lambda b, i: (0, b * nq + i)
    return pl.pallas_call(
        functools.partial(_dsa_body, topk),
        grid=(batch, nq),
        in_specs=[
            pl.BlockSpec(memory_space=pltpu.SMEM),
            pl.BlockSpec((IDX_HEADS * IDX_DIM, TQ), col),
            pl.BlockSpec((TQ, LANES), lambda b, i: (b * nq + i, 0)),
            pl.BlockSpec((nh, TQ), col),
            pl.BlockSpec((seq, IDX_DIM), lambda b, i: (b, 0), pipeline_mode=pl.Buffered(1)),
            pl.BlockSpec((seq, nkv), lambda b, i: (b, 0), pipeline_mode=pl.Buffered(1)),
            pl.BlockSpec((nkv, seq), lambda b, i: (0, b), pipeline_mode=pl.Buffered(1)),
        ],
        out_specs=pl.BlockSpec((TQ, nh), lambda b, i: (b * nq + i, 0)),
        out_shape=jax.ShapeDtypeStruct((t, nh), BF16),
        scratch_shapes=[
            pltpu.VMEM((seq, TQ), jnp.int32),
            pltpu.VMEM((seq, TQ), jnp.int16),
            pltpu.VMEM((seq, TQ), jnp.int16),
            pltpu.VMEM((LANES, TQ), F32),
            pltpu.VMEM((DSA_KV_HEADS, HEAD_DIM, grp * TQ), BF16),
            pltpu.VMEM((DSA_KV_HEADS, 1, grp * TQ), F32),
            pltpu.VMEM((DSA_KV_HEADS, 1, grp * TQ), F32),
            pltpu.VMEM((DSA_KV_HEADS, HEAD_DIM, grp * TQ), F32),
        ],
        compiler_params=_params("parallel", "arbitrary"),
        name="dsa",
    )(bound, iqt, iw, dqt, ik, dk, vt)


def _merge_body(og_ref, od_ref, wg_ref, wd_ref, gg_ref, gd_ref, bgg_ref, bgd_ref, o_ref):
    yg = _dot(og_ref[...], wg_ref[...])
    yd = _dot(od_ref[...], wd_ref[...])
    sg = 1.0 / (1.0 + jnp.exp(-(gg_ref[...] + bgg_ref[...])))
    sd = 1.0 / (1.0 + jnp.exp(-(gd_ref[...] + bgd_ref[...])))
    o_ref[...] = (sg * yg + sd * yd).astype(o_ref.dtype)


def _merge(o_gla, o_dsa, wg, wd, p_gate, b_gate, tm=1024, tn=512):
    t, dg = o_gla.shape
    dd = o_dsa.shape[1]
    d = wg.shape[1]
    nb = d // tn
    return pl.pallas_call(
        _merge_body,
        grid=(t // tm, nb),
        in_specs=[
            pl.BlockSpec((tm, dg), lambda i, j: (i, 0)),
            pl.BlockSpec((tm, dd), lambda i, j: (i, 0)),
            pl.BlockSpec((dg, tn), lambda i, j: (0, j)),
            pl.BlockSpec((dd, tn), lambda i, j: (0, j)),
            pl.BlockSpec((tm, tn), lambda i, j: (i, j)),
            pl.BlockSpec((tm, tn), lambda i, j: (i, nb + j)),
            pl.BlockSpec((1, tn), lambda i, j: (0, j)),
            pl.BlockSpec((1, tn), lambda i, j: (0, nb + j)),
        ],
        out_specs=pl.BlockSpec((tm, tn), lambda i, j: (i, j)),
        out_shape=jax.ShapeDtypeStruct((t, d), BF16),
        compiler_params=_params("parallel", "parallel"),
        name="merge",
    )(o_gla, o_dsa, wg, wd, p_gate, p_gate, b_gate, b_gate)


def _outproj_body(x_ref, m_ref, w_ref, o_ref):
    o_ref[...] = x_ref[...] + _dot(m_ref[...], w_ref[...])


def _outproj(x2, m, w_out, tm=1024, tn=1024):
    t, d = x2.shape
    tn = min(tn, d)
    return pl.pallas_call(
        _outproj_body,
        grid=(t // tm, d // tn),
        in_specs=[
            pl.BlockSpec((tm, tn), lambda i, j: (i, j)),
            pl.BlockSpec((tm, d), lambda i, j: (i, 0)),
            pl.BlockSpec((d, tn), lambda i, j: (0, j)),
        ],
        out_specs=pl.BlockSpec((tm, tn), lambda i, j: (i, j)),
        out_shape=jax.ShapeDtypeStruct((t, d), F32),
        compiler_params=_params("parallel", "parallel"),
        name="out_proj",
    )(x2, m, w_out)


def _mlp_body(x_ref, g_ref, w1_ref, w2_ref, o_ref, h_scr):
    @pl.when(pl.program_id(1) == 0)
    def _():
        x = x_ref[...]
        h_scr[...] = (x * lax.rsqrt(jnp.mean(x * x, axis=-1, keepdims=True) + NORM_EPS) * g_ref[...]).astype(BF16)
        o_ref[...] = x

    u = jnp.maximum(_dot(h_scr[...], w1_ref[...]), 0.0)
    o_ref[...] += _dot((u * u).astype(BF16), w2_ref[...])


def _mlp(x1, g, w1, w2, tm=512, tf=1024):
    t, d = x1.shape
    ff = w1.shape[1]
    return pl.pallas_call(
        _mlp_body,
        grid=(t // tm, ff // tf),
        in_specs=[
            pl.BlockSpec((tm, d), lambda i, f: (i, 0)),
            pl.BlockSpec((1, d), lambda i, f: (0, 0)),
            pl.BlockSpec((d, tf), lambda i, f: (0, f)),
            pl.BlockSpec((tf, d), lambda i, f: (f, 0)),
        ],
        out_specs=pl.BlockSpec((tm, d), lambda i, f: (i, 0)),
        out_shape=jax.ShapeDtypeStruct((t, d), F32),
        scratch_shapes=[pltpu.VMEM((tm, d), BF16)],
        compiler_params=_params("parallel", "arbitrary"),
        name="mlp",
    )(x1, g.reshape(1, d), w1, w2)


def _rope_tables(seq, q_g, k_g, ik_g):
    pos = jnp.arange(seq, dtype=jnp.int32).astype(F32)

    def cs(rot):
        half = rot // 2
        inv = ROPE_THETA ** (-jnp.arange(half, dtype=F32) * 2.0 / rot)
        ang = pos[:, None] * inv[None, :]
        return jnp.cos(ang), jnp.sin(ang)

    c, s = cs(HEAD_DIM)
    hh = HEAD_DIM // 2
    ka = jnp.concatenate([c, c], axis=1) * k_g[None, :]
    kb = jnp.concatenate([-s * k_g[None, hh:], s * k_g[None, :hh]], axis=1)
    qs = (HEAD_DIM ** -0.5) * LOG2E
    qa = (jnp.concatenate([c, c], axis=1) * q_g[None, :] * qs).T
    qb = (jnp.concatenate([-s * q_g[None, hh:], s * q_g[None, :hh]], axis=1) * qs).T

    ci, si = cs(IDX_ROPE_DIM)
    h2 = IDX_ROPE_DIM // 2
    one = jnp.ones((seq, IDX_DIM - IDX_ROPE_DIM), F32)
    zero = jnp.zeros((seq, IDX_DIM - IDX_ROPE_DIM), F32)
    zh = jnp.zeros((seq, h2), F32)
    ia = jnp.concatenate([ci, ci, one], axis=1) * ik_g[None, :]
    ib1 = jnp.concatenate([-si * ik_g[None, h2:2 * h2], zh, zero], axis=1)
    ib2 = jnp.concatenate([zh, si * ik_g[None, :h2], zero], axis=1)
    iqs = IDX_DIM ** -0.5
    iqa = (jnp.concatenate([ci, ci, one], axis=1) * iqs).T
    iqb = (jnp.concatenate([-si, si, zero], axis=1) * iqs).T
    return (ka, kb, ia, ib1, ib2), (qa, qb), (iqa, iqb)


def kernel(x, norm1_g, w_in, gla_wg2, gla_bg, gla_norm_g, w_proj_gla, q_norm_g, k_norm_g, idx_k_norm_g,
           w_proj_dsa, b_gate, w_out, norm2_g, w_ff1, w_ff2):
    batch, seq, d = x.shape
    depth = w_in.shape[0]
    topk = min(TOPK_MAX, seq // 4)
    nqk = GLA_HEADS * GLA_DK
    nv = GLA_HEADS * GLA_DV
    ndq = DSA_HEADS * HEAD_DIM
    nkv = DSA_KV_HEADS * HEAD_DIM
    niq = IDX_HEADS * IDX_DIM
    splits = (nqk, nqk, nv, nv, GLA_GATE_RANK, ndq, nkv, nkv, niq, IDX_DIM, IDX_HEADS, 2 * d)
    offs = [0]
    for n in splits:
        offs.append(offs[-1] + n)
    o_gq, _, _, _, o_glr, o_dq, o_dk, o_dv, o_iq, o_ik, o_iw, o_gate, _ = offs
    assert (2 * nqk + 2 * nv) % 1024 == 0 and seq % 512 == 0 and d % 512 == 0

    x2 = x.reshape(batch * seq, d)
    for l in range(depth):
        w_gla, w_gate, w_k, wt_dq, wt_iq, wt_v = _repack_w_in(
            w_in[l], (o_gq, o_glr, o_dq, o_dk, o_dv, o_iq, o_ik, o_iw, o_gate, offs[-1]))
        wg2p = jnp.concatenate(
            [gla_wg2[l], jnp.zeros((LANES - GLA_GATE_RANK, nqk), F32)], axis=0).astype(BF16)
        k_tabs, q_tabs, iq_tabs = _rope_tables(seq, q_norm_g[l], k_norm_g[l], idx_k_norm_g[l])

        h, ht = _rmsnorm_both(x2, norm1_g[l])
        p_gla = _mm(h, w_gla, F32, 1024, 1024, "proj_gla")
        p_gate = _mm(h, w_gate, F32, 1024, 1024, "proj_gate")
        dk, ik, glr, iw = _proj_k(h, w_k, k_tabs, seq)
        dqt = _proj_t("dq", wt_dq, ht, q_tabs, seq, BF16, 1024)
        iqt = _proj_t("iq", wt_iq, ht, iq_tabs, seq, BF16, 1024)
        vt = _proj_t("v", wt_v, ht, (), seq, BF16, nkv)

        b_cum, drop = _gla_decay(glr, wg2p, gla_bg[l].reshape(1, nqk))
        chunks_per_tile = batch * seq // GLA_CHUNK // drop.shape[0]
        mild = (drop[:, :chunks_per_tile, 0].reshape(-1) <= GLA_MAX_SPLIT_LOG2).astype(jnp.int32)
        o_gla = _gla(p_gla, b_cum, mild, gla_norm_g[l].reshape(1, GLA_DV), batch, seq)
        bound = (1.02 * HEAD_DIM * (HEAD_DIM ** -0.5) * LOG2E
                 * jnp.max(jnp.abs(q_norm_g[l])) * jnp.max(jnp.abs(k_norm_g[l]))).reshape(1, 1).astype(F32)
        o_dsa = _dsa(bound, iqt, iw, dqt, ik, dk, vt, batch, seq, topk)

        m = _merge(o_gla, o_dsa, w_proj_gla[l].astype(BF16), w_proj_dsa[l].astype(BF16), p_gate,
                   b_gate[l].reshape(1, 2 * d))
        x1 = _outproj(x2, m, w_out[l].astype(BF16))
        x2 = _mlp(x1, norm2_g[l], w_ff1[l].astype(BF16), w_ff2[l].astype(BF16))
    return x2.reshape(batch, seq, d)
```

```python
import functools
import math

import jax
import jax.numpy as jnp
from jax import lax
from jax.experimental import pallas as pl
from jax.experimental.pallas import tpu as pltpu

BF16 = jnp.bfloat16
F32 = jnp.float32

NORM_EPS = 1e-6
ROPE_THETA = 10000.0
GLA_HEADS = 4
GLA_DK = 256
GLA_DV = 512
GLA_GATE_RANK = 16
GLA_GATE_NORMALIZER = 16.0
DSA_HEADS = 16
DSA_KV_HEADS = 4
HEAD_DIM = 128
IDX_HEADS = 16
IDX_DIM = 128
IDX_ROPE_DIM = 64
TOPK_MAX = 256

LANES = 128
VMEM_LIMIT = 56 * 1024 * 1024
GLA_CHUNK = 128
GLA_SUB = 16
GLA_MAX_SPLIT_LOG2 = 100.0
DSA_TQ = 256
DSA_TK = 512
DSA_ATK = 512
DSA_COUNT_ROWS = 64
DSA_MIN_DENOM = 2.0 ** -80
INT_MIN = -(2 ** 31)
HALF_BIAS = 2 ** 15
NEG_BIG = -1e30
LOG2E = math.log2(math.e)


def _loop_by_pairs(n, body):
    def pair(jj, carry):
        body(2 * jj)
        body(2 * jj + 1)
        return carry

    lax.fori_loop(0, n // 2, pair, 0)

    @pl.when(n % 2 == 1)
    def _():
        body(n - 1)


def _params(*sem):
    return pltpu.CompilerParams(dimension_semantics=sem, vmem_limit_bytes=VMEM_LIMIT)


def _dot(a, b):
    return jnp.dot(a, b, preferred_element_type=F32)


def _repack_body(offs, w_ref, gla_ref, gate_ref, k_ref, tdq_ref, tiq_ref, tv_ref):
    o_gq, o_glr, o_dq, o_dk, o_dv, o_iq, o_ik, o_iw, o_gate, o_end = offs
    gla_ref[...] = w_ref[:, o_gq:o_glr].astype(BF16)
    gate_ref[...] = w_ref[:, o_gate:o_end].astype(BF16)
    tdq_ref[...] = w_ref[:, o_dq:o_dk].T.astype(BF16)
    tiq_ref[...] = w_ref[:, o_iq:o_ik].T.astype(BF16)
    tv_ref[...] = w_ref[:, o_dv:o_iq].T.astype(BF16)
    lane = lax.broadcasted_iota(jnp.int32, (w_ref.shape[0], LANES), 1)

    def narrow(start, width):
        return jnp.where(lane < width, w_ref[:, start:start + LANES], 0.0).astype(BF16)

    nkv = o_dv - o_dk
    k_ref[:, :nkv] = w_ref[:, o_dk:o_dv].astype(BF16)
    k_ref[:, nkv:nkv + IDX_DIM] = w_ref[:, o_ik:o_iw].astype(BF16)
    k_ref[:, nkv + IDX_DIM:nkv + IDX_DIM + LANES] = narrow(o_glr, GLA_GATE_RANK)
    k_ref[:, nkv + IDX_DIM + LANES:] = narrow(o_iw, IDX_HEADS)


def _repack_w_in(w, offs, tr=128):
    d, ncol = w.shape
    o_gq, o_glr, o_dq, o_dk, o_dv, o_iq, o_ik, o_iw, o_gate, o_end = offs
    n_gla, n_gate, n_dq, n_iq, n_v = o_glr - o_gq, o_end - o_gate, o_dk - o_dq, o_ik - o_iq, o_iq - o_dv
    n_k = (o_dv - o_dk) + IDX_DIM + 2 * LANES
    row = lambda n: pl.BlockSpec((tr, n), lambda i: (i, 0))
    col = lambda n: pl.BlockSpec((n, tr), lambda i: (0, i))
    sds = jax.ShapeDtypeStruct
    return pl.pallas_call(
        functools.partial(_repack_body, offs),
        grid=(d // tr,),
        in_specs=[row(ncol)],
        out_specs=[row(n_gla), row(n_gate), row(n_k), col(n_dq), col(n_iq), col(n_v)],
        out_shape=[sds((d, n_gla), BF16), sds((d, n_gate), BF16), sds((d, n_k), BF16),
                   sds((n_dq, d), BF16), sds((n_iq, d), BF16), sds((n_v, d), BF16)],
        compiler_params=_params("parallel"),
        name="repack_w_in",
    )(w)


def _rms_body(x_ref, g_ref, h_ref, ht_ref):
    x = x_ref[...]
    y = x * lax.rsqrt(jnp.mean(x * x, axis=-1, keepdims=True) + NORM_EPS) * g_ref[...]
    h_ref[...] = y.astype(BF16)
    ht_ref[...] = y.T.astype(BF16)


def _rmsnorm_both(x2, g, tm=256):
    t, d = x2.shape
    return pl.pallas_call(
        _rms_body,
        grid=(t // tm,),
        in_specs=[pl.BlockSpec((tm, d), lambda i: (i, 0)), pl.BlockSpec((1, d), lambda i: (0, 0))],
        out_specs=[pl.BlockSpec((tm, d), lambda i: (i, 0)), pl.BlockSpec((d, tm), lambda i: (0, i))],
        out_shape=[jax.ShapeDtypeStruct((t, d), BF16), jax.ShapeDtypeStruct((d, t), BF16)],
        compiler_params=_params("parallel"),
        name="rmsnorm_in",
    )(x2, g.reshape(1, d))


def _mm_body(a_ref, w_ref, o_ref):
    o_ref[...] = _dot(a_ref[...], w_ref[...]).astype(o_ref.dtype)


def _mm(a, w, out_dtype, tm, tn, name):
    m, k = a.shape
    n = w.shape[1]
    return pl.pallas_call(
        _mm_body,
        grid=(m // tm, n // tn),
        in_specs=[pl.BlockSpec((tm, k), lambda i, j: (i, 0)), pl.BlockSpec((k, tn), lambda i, j: (0, j))],
        out_specs=pl.BlockSpec((tm, tn), lambda i, j: (i, j)),
        out_shape=jax.ShapeDtypeStruct((m, n), out_dtype),
        compiler_params=_params("parallel", "parallel"),
        name=name,
    )(a, w)


def _projk_body(h_ref, w_ref, ka_ref, kb_ref, ia_ref, ib1_ref, ib2_ref, dk_ref, ik_ref, glr_ref, iw_ref):
    p = _dot(h_ref[...], w_ref[...])
    nkv = DSA_KV_HEADS
    for j in range(nkv):
        pj = p[:, j * HEAD_DIM:(j + 1) * HEAD_DIM]
        n = pj * lax.rsqrt(jnp.mean(pj * pj, axis=-1, keepdims=True) + NORM_EPS)
        out = n * ka_ref[...] + pltpu.roll(n, HEAD_DIM // 2, 1) * kb_ref[...]
        dk_ref[:, j * HEAD_DIM:(j + 1) * HEAD_DIM] = out.astype(BF16)
    c0 = nkv * HEAD_DIM
    pi = p[:, c0:c0 + IDX_DIM]
    n = pi * lax.rsqrt(jnp.mean(pi * pi, axis=-1, keepdims=True) + NORM_EPS)
    half = IDX_ROPE_DIM // 2
    out = n * ia_ref[...] + pltpu.roll(n, IDX_DIM - half, 1) * ib1_ref[...] + pltpu.roll(n, half, 1) * ib2_ref[...]
    ik_ref[...] = out.astype(BF16)
    c1 = c0 + IDX_DIM
    glr_ref[...] = p[:, c1:c1 + LANES]
    iw_ref[...] = p[:, c1 + LANES:] * (IDX_HEADS ** -0.5)


def _proj_k(h, w_k, tabs, seq, tm=512):
    t, d = h.shape
    nblk = seq // tm
    ncol = w_k.shape[1]
    tab_spec = pl.BlockSpec((tm, LANES), lambda i: (i % nblk, 0))
    return pl.pallas_call(
        _projk_body,
        grid=(t // tm,),
        in_specs=[pl.BlockSpec((tm, d), lambda i: (i, 0)), pl.BlockSpec((d, ncol), lambda i: (0, 0))] + [tab_spec] * 5,
        out_specs=[pl.BlockSpec((tm, DSA_KV_HEADS * HEAD_DIM), lambda i: (i, 0)),
                   pl.BlockSpec((tm, IDX_DIM), lambda i: (i, 0)),
                   pl.BlockSpec((tm, LANES), lambda i: (i, 0)),
                   pl.BlockSpec((tm, LANES), lambda i: (i, 0))],
        out_shape=[jax.ShapeDtypeStruct((t, DSA_KV_HEADS * HEAD_DIM), BF16),
                   jax.ShapeDtypeStruct((t, IDX_DIM), BF16),
                   jax.ShapeDtypeStruct((t, LANES), F32),
                   jax.ShapeDtypeStruct((t, LANES), F32)],
        compiler_params=_params("parallel"),
        name="proj_k",
    )(h, w_k, *tabs)


def _projt_body(kind, w_ref, ht_ref, *rest):
    p = _dot(w_ref[...], ht_ref[...])
    o_ref = rest[-1]
    rows = p.shape[0]
    if kind == "v":
        o_ref[...] = p.astype(o_ref.dtype)
        return
    ta_ref, tb_ref = rest[0], rest[1]
    for j in range(rows // HEAD_DIM):
        pj = p[j * HEAD_DIM:(j + 1) * HEAD_DIM]
        if kind == "dq":
            n = pj * lax.rsqrt(jnp.mean(pj * pj, axis=0, keepdims=True) + NORM_EPS)
            sw = jnp.concatenate([n[HEAD_DIM // 2:], n[:HEAD_DIM // 2]], axis=0)
        else:
            n = pj
            half = IDX_ROPE_DIM // 2
            sw = jnp.concatenate([n[half:2 * half], n[:half], n[2 * half:]], axis=0)
        o_ref[j * HEAD_DIM:(j + 1) * HEAD_DIM, :] = (n * ta_ref[...] + sw * tb_ref[...]).astype(o_ref.dtype)


def _proj_t(kind, w_t, ht, tabs, seq, out_dtype, tc, tt=512):
    c, d = w_t.shape
    t = ht.shape[1]
    nblk = seq // tt
    tab_spec = pl.BlockSpec((HEAD_DIM, tt), lambda i, j: (0, j % nblk))
    return pl.pallas_call(
        functools.partial(_projt_body, kind),
        grid=(c // tc, t // tt),
        in_specs=[pl.BlockSpec((tc, d), lambda i, j: (i, 0)), pl.BlockSpec((d, tt), lambda i, j: (0, j))]
        + [tab_spec] * len(tabs),
        out_specs=pl.BlockSpec((tc, tt), lambda i, j: (i, j)),
        out_shape=jax.ShapeDtypeStruct((c, t), out_dtype),
        compiler_params=_params("parallel", "parallel"),
        name="proj_t_" + kind,
    )(w_t, ht, *tabs)


def _gla_scores_explicit(q, k, b, bt):
    C, SB = GLA_CHUNK, GLA_SUB
    kt = k.T
    lane = lax.broadcasted_iota(jnp.int32, (SB, C), 1)
    rowi = lax.broadcasted_iota(jnp.int32, (SB, 1), 0)
    blocks = []
    for blk in range(C // SB):
        s = blk * SB
        q_i = q[s:s + SB]
        b_i = b[s:s + SB]
        a = jnp.zeros((SB, C), F32)
        for j in range(SB):
            r = s + j
            dec = jnp.where(rowi >= j, jnp.exp2(b_i - b[r:r + 1, :]), 0.0)
            col = jnp.sum(q_i * k[r:r + 1, :] * dec, axis=-1, keepdims=True)
            a = jnp.where(lane == r, col, a)
        if blk > 0:
            q_s = q_i * jnp.exp2(b_i - b[s - 1:s, :])
            k_s = kt * jnp.exp2(jnp.minimum(bt[:, s - 1:s] - bt, 0.0))
            a = jnp.where(lane < s, _dot(q_s.astype(BF16), k_s.astype(BF16)), a)
        blocks.append(a)
    return jnp.concatenate(blocks, axis=0)


def _gla_scores_split(q, k, b):
    C, SB = GLA_CHUNK, GLA_SUB
    nt = (((1,), (1,)), ((), ()))
    ri = lax.broadcasted_iota(jnp.int32, (C, C), 0)
    ci = lax.broadcasted_iota(jnp.int32, (C, C), 1)
    zeros = lambda n: jnp.zeros((n, GLA_DK), F32)
    a = jnp.zeros((C, C), F32)
    span = C // 2
    while span >= SB:
        q_parts, k_parts = [], []
        for lo in range(0, C, 2 * span):
            mid, hi = lo + span, lo + 2 * span
            ref = b[mid - 1:mid, :]
            k_parts += [k[lo:mid] * jnp.exp2(ref - b[lo:mid]), zeros(span)]
            q_parts += [zeros(span), q[mid:hi] * jnp.exp2(b[mid:hi] - ref)]
        s_l = lax.dot_general(jnp.concatenate(q_parts, axis=0).astype(BF16),
                              jnp.concatenate(k_parts, axis=0).astype(BF16), nt, preferred_element_type=F32)
        shift = (2 * span).bit_length() - 1
        a = a + jnp.where(jnp.right_shift(ri, shift) == jnp.right_shift(ci, shift), s_l, 0.0)
        span //= 2
    q_parts, k_parts = [], []
    for lo in range(0, C, SB):
        gap = b[lo:lo + SB] - b[lo - 1:lo, :] if lo > 0 else b[lo:lo + SB]
        q_parts.append(q[lo:lo + SB] * jnp.exp2(gap))
        k_parts.append(k[lo:lo + SB] * jnp.exp2(jnp.minimum(-gap, GLA_MAX_SPLIT_LOG2)))
    s_d = lax.dot_general(jnp.concatenate(q_parts, axis=0).astype(BF16),
                          jnp.concatenate(k_parts, axis=0).astype(BF16), nt, preferred_element_type=F32)
    shift = SB.bit_length() - 1
    same = jnp.right_shift(ri, shift) == jnp.right_shift(ci, shift)
    return a + jnp.where(same, jnp.where(ci <= ri, s_d, 0.0), 0.0)


def _gla_head(h, explicit, q_ref, k_ref, v_ref, gr_ref, ng_ref, o_ref, st_ref, b_ref):
    C = GLA_CHUNK
    qc = slice(h * GLA_DK, (h + 1) * GLA_DK)
    vc = slice(h * GLA_DV, (h + 1) * GLA_DV)
    q = q_ref[:, qc] * (GLA_DK ** -0.5)
    k = k_ref[:, qc]
    vb = v_ref[:, vc].astype(BF16)
    b = b_ref[:, qc]
    bt = b.T

    o = _dot((q * jnp.exp2(b)).astype(BF16), st_ref[h].astype(BF16))
    a = _gla_scores_explicit(q, k, b, bt) if explicit else _gla_scores_split(q, k, b)
    o = o + _dot(a.astype(BF16), vb)

    k_dec = k * jnp.exp2(b[C - 1:C, :] - b)
    st_ref[h] = st_ref[h] * jnp.exp2(bt[:, C - 1:C]) + _dot(k_dec.T.astype(BF16), vb)

    on = o * lax.rsqrt(jnp.mean(o * o, axis=-1, keepdims=True) + NORM_EPS) * ng_ref[...]
    gr = gr_ref[:, vc]
    o_ref[:, vc] = (on * gr * (1.0 / (1.0 + jnp.exp(-gr)))).astype(o_ref.dtype)


def _decay_body(glr_ref, wg2_ref, bg_ref, b_ref, drop_ref):
    C, SB = GLA_CHUNK, GLA_SUB
    z = _dot(glr_ref[...].astype(BF16), wg2_ref[...]) + bg_ref[...]
    g = (jnp.minimum(z, 0.0) - jnp.log1p(jnp.exp(-jnp.abs(z)))) * (LOG2E / GLA_GATE_NORMALIZER)
    ri = lax.broadcasted_iota(jnp.int32, (C, C), 0)
    ci = lax.broadcasted_iota(jnp.int32, (C, C), 1)
    tri = jnp.where(ci <= ri, 1.0, 0.0).astype(BF16)
    g1 = g.astype(BF16)
    r1 = g - g1.astype(F32)
    g2 = r1.astype(BF16)
    g3 = (r1 - g2.astype(F32)).astype(BF16)
    drop_ref[...] = jnp.zeros_like(drop_ref)
    for ch in range(glr_ref.shape[0] // C):
        rows = slice(ch * C, (ch + 1) * C)
        b = _dot(tri, g1[rows]) + _dot(tri, g2[rows]) + _dot(tri, g3[rows])
        b_ref[rows, :] = b
        drop = -b[SB - 1:SB, :]
        for blk in range(1, C // SB):
            s = blk * SB
            drop = jnp.maximum(drop, b[s - 1:s, :] - b[s + SB - 1:s + SB, :])
        drop_ref[0, ch:ch + 1, :] = jnp.broadcast_to(jnp.max(drop, axis=-1, keepdims=True), (1, LANES))


def _gla_decay(glr, wg2p, bg, tm=512):
    t = glr.shape[0]
    nqk = wg2p.shape[1]
    assert tm // GLA_CHUNK <= 8
    return pl.pallas_call(
        _decay_body,
        grid=(t // tm,),
        in_specs=[
            pl.BlockSpec((tm, LANES), lambda i: (i, 0)),
            pl.BlockSpec((LANES, nqk), lambda i: (0, 0)),
            pl.BlockSpec((1, nqk), lambda i: (0, 0)),
        ],
        out_specs=[pl.BlockSpec((tm, nqk), lambda i: (i, 0)), pl.BlockSpec((1, 8, LANES), lambda i: (i, 0, 0))],
        out_shape=[jax.ShapeDtypeStruct((t, nqk), F32), jax.ShapeDtypeStruct((t // tm, 8, LANES), F32)],
        compiler_params=_params("parallel"),
        name="gla_decay",
    )(glr, wg2p, bg)


def _gla_body(mild_ref, q_ref, k_ref, v_ref, gr_ref, b_ref, ng_ref, o_ref, st_ref):
    @pl.when(pl.program_id(1) == 0)
    def _():
        st_ref[...] = jnp.zeros_like(st_ref)

    refs = (q_ref, k_ref, v_ref, gr_ref, ng_ref, o_ref, st_ref, b_ref)
    mild = mild_ref[pl.program_id(0) * pl.num_programs(1) + pl.program_id(1)] == 1

    @pl.when(mild)
    def _():
        for h in range(GLA_HEADS):
            _gla_head(h, False, *refs)

    @pl.when(jnp.logical_not(mild))
    def _():
        for h in range(GLA_HEADS):
            _gla_head(h, True, *refs)


def _gla(p_gla, b_cum, mild, ng, batch, seq):
    C = GLA_CHUNK
    nc = seq // C
    t = batch * seq
    nqk = GLA_HEADS * GLA_DK
    nv = GLA_HEADS * GLA_DV
    vblk = 2 * nqk // nv
    row = lambda b, c: b * nc + c
    return pl.pallas_call(
        _gla_body,
        grid_spec=pltpu.PrefetchScalarGridSpec(
            num_scalar_prefetch=1,
            grid=(batch, nc),
            in_specs=[
                pl.BlockSpec((C, nqk), lambda b, c, m: (row(b, c), 0)),
                pl.BlockSpec((C, nqk), lambda b, c, m: (row(b, c), 1)),
                pl.BlockSpec((C, nv), lambda b, c, m: (row(b, c), vblk)),
                pl.BlockSpec((C, nv), lambda b, c, m: (row(b, c), vblk + 1)),
                pl.BlockSpec((C, nqk), lambda b, c, m: (row(b, c), 0)),
                pl.BlockSpec((1, GLA_DV), lambda b, c, m: (0, 0)),
            ],
            out_specs=pl.BlockSpec((C, nv), lambda b, c, m: (row(b, c), 0)),
            scratch_shapes=[pltpu.VMEM((GLA_HEADS, GLA_DK, GLA_DV), F32)],
        ),
        out_shape=jax.ShapeDtypeStruct((t, nv), BF16),
        compiler_params=_params("parallel", "arbitrary"),
        name="gla",
    )(mild, p_gla, p_gla, p_gla, p_gla, b_cum, ng)


def _dsa_body(topk, bound_ref, iqt_ref, iw_ref, dqt_ref, ik_ref, dk_ref, vt_ref, o_ref,
              sc_ref, hi_ref, lo_ref, iwt_ref, qg_ref, m_ref, l_ref, acc_ref):
    TQ, TK = DSA_TQ, DSA_TK
    i = pl.program_id(1)
    nk = (i * TQ + TQ + TK - 1) // TK
    q_pos = i * TQ + lax.broadcasted_iota(jnp.int32, (TK, TQ), 1)
    k_off = lax.broadcasted_iota(jnp.int32, (TK, TQ), 0)
    iwt_ref[...] = iw_ref[...].T

    def score_tile(j, carry):
        base = pl.multiple_of(j * TK, TK)
        ikt = ik_ref[pl.ds(base, TK), :]
        acc = jnp.zeros((TK, TQ), F32)
        for h in range(IDX_HEADS):
            r = _dot(ikt, iqt_ref[h * IDX_DIM:(h + 1) * IDX_DIM, :])
            acc = acc + jnp.maximum(r, 0.0) * iwt_ref[h:h + 1, :]
        bits = pltpu.bitcast(acc, jnp.int32)
        key = jnp.where(bits < 0, bits ^ jnp.int32(0x7FFFFFFF), bits)
        key = jnp.where(base + k_off <= q_pos, key, jnp.int32(INT_MIN))
        sc_ref[pl.ds(base, TK), :] = key
        hi_ref[pl.ds(base, TK), :] = jnp.right_shift(key, 16).astype(jnp.int16)
        lo_ref[pl.ds(base, TK), :] = ((key & jnp.int32(0xFFFF)) - jnp.int32(HALF_BIAS)).astype(jnp.int16)
        return carry

    _loop_by_pairs(nk, lambda j: score_tile(j, 0))

    CR = DSA_COUNT_ROWS

    def count_ge16(ref, cand16):
        def body(j, cnt):
            base = pl.multiple_of(j * TK, TK)
            hit = jnp.where(ref[pl.ds(base, TK), :] >= cand16, jnp.int16(1), jnp.int16(0))
            for r in range(TK // CR):
                cnt = cnt + hit[r * CR:(r + 1) * CR]
            return cnt
        cnt = lax.fori_loop(0, nk, body, jnp.zeros((CR, TQ), jnp.int16))
        return jnp.sum(cnt.astype(jnp.int32), axis=0, keepdims=True)

    def kth_largest16(ref, rank):
        zero = jnp.zeros((1, TQ), jnp.int32)
        c0 = count_ge16(ref, zero.astype(jnp.int16))
        ok0 = c0 >= rank
        state0 = (jnp.where(ok0, zero, jnp.int32(-HALF_BIAS)), jnp.where(ok0, c0, nk * TK))

        def bit_step(it, state):
            t, cnt = state
            cand = t | jnp.left_shift(jnp.int32(1), 14 - it)
            c = count_ge16(ref, cand.astype(jnp.int16))
            ok = c >= rank
            return jnp.where(ok, cand, t), jnp.where(ok, c, cnt)

        return lax.fori_loop(0, 15, bit_step, state0)

    def count_above16(ref, t):
        c = count_ge16(ref, jnp.minimum(t + 1, HALF_BIAS - 1).astype(jnp.int16))
        return jnp.where(t >= HALF_BIAS - 1, 0, c)

    t_hi, n_hi = kth_largest16(hi_ref, topk)
    above = count_above16(hi_ref, t_hi)
    t_hi16 = t_hi.astype(jnp.int16)

    def low_tile(j, carry):
        base = pl.multiple_of(j * TK, TK)
        rows = pl.ds(base, TK)
        lo_ref[rows, :] = jnp.where(hi_ref[rows, :] == t_hi16, lo_ref[rows, :], jnp.int16(-HALF_BIAS))
        return carry

    lax.fori_loop(0, nk, low_tile, 0)
    t_lo, n_lo = kth_largest16(lo_ref, topk - above)
    thr = jnp.left_shift(t_hi, 16) + (t_lo + HALF_BIAS)

    n_lo = jnp.where(t_lo == -HALF_BIAS, n_hi - above, n_lo)
    n_adm = i * TQ + 1 + lax.broadcasted_iota(jnp.int32, (1, TQ), 1)
    surplus = jnp.where(n_adm > topk, above + n_lo - topk, 0)

    @pl.when(jnp.max(surplus) > 0)
    def _():
        quota = jnp.where(surplus > 0, topk - above - count_above16(lo_ref, t_lo), nk * TK).astype(F32)
        before = (lax.broadcasted_iota(jnp.int32, (TK, TK), 1)
                  < lax.broadcasted_iota(jnp.int32, (TK, TK), 0))
        before = jnp.where(before, 1.0, 0.0).astype(BF16)

        def tie_tile(j, seen):
            base = pl.multiple_of(j * TK, TK)
            key = sc_ref[pl.ds(base, TK), :]
            tie = jnp.where(key == thr, 1.0, 0.0)
            rank = seen + _dot(before, tie.astype(BF16))
            sc_ref[pl.ds(base, TK), :] = jnp.where((tie > 0.0) & (rank >= quota), jnp.int32(INT_MIN), key)
            return seen + jnp.sum(tie, axis=0, keepdims=True)

        lax.fori_loop(0, nk, tie_tile, jnp.zeros((1, TQ), F32))

    thr = jnp.maximum(thr, jnp.int32(INT_MIN + 1))

    grp = DSA_HEADS // DSA_KV_HEADS
    AK = DSA_ATK
    na = (i * TQ + TQ + AK - 1) // AK
    for g in range(DSA_KV_HEADS):
        qg_ref[g] = jnp.concatenate(
            [dqt_ref[(g * grp + hh) * HEAD_DIM:(g * grp + hh + 1) * HEAD_DIM, :] for hh in range(grp)], axis=1)

    def tile_operands(j, g):
        base = pl.multiple_of(j * AK, AK)
        kt = dk_ref[pl.ds(base, AK), g * HEAD_DIM:(g + 1) * HEAD_DIM]
        vt = vt_ref[g * HEAD_DIM:(g + 1) * HEAD_DIM, pl.ds(base, AK)]
        return kt, vt

    def tile_bias(j, selected):
        base = pl.multiple_of(j * AK, AK)
        bias = jnp.where(sc_ref[pl.ds(base, AK), :] >= thr, selected, NEG_BIG)
        return jnp.concatenate([bias] * grp, axis=1)

    l_ref[...] = jnp.zeros_like(l_ref)
    acc_ref[...] = jnp.zeros_like(acc_ref)
    neg_bound = -bound_ref[0, 0]

    def bounded_tile(j, carry):
        bias = tile_bias(j, neg_bound)
        for g in range(DSA_KV_HEADS):
            kt, vt = tile_operands(j, g)
            p = jnp.exp2(_dot(kt, qg_ref[g]) + bias)
            l_ref[g] += jnp.sum(p, axis=0, keepdims=True)
            acc_ref[g] += _dot(vt, p.astype(BF16))
        return carry

    _loop_by_pairs(na, lambda j: bounded_tile(j, 0))

    @pl.when(jnp.logical_not(jnp.min(l_ref[...]) >= DSA_MIN_DENOM))
    def _():
        m_ref[...] = jnp.full_like(m_ref, NEG_BIG)
        l_ref[...] = jnp.zeros_like(l_ref)
        acc_ref[...] = jnp.zeros_like(acc_ref)

        def online_tile(j, carry):
            bias = tile_bias(j, 0.0)
            for g in range(DSA_KV_HEADS):
                kt, vt = tile_operands(j, g)
                s = _dot(kt, qg_ref[g]) + bias
                m_old = m_ref[g]
                m_new = jnp.maximum(m_old, jnp.max(s, axis=0, keepdims=True))
                p = jnp.exp2(s - m_new)
                alpha = jnp.exp2(m_old - m_new)
                l_ref[g] = alpha * l_ref[g] + jnp.sum(p, axis=0, keepdims=True)
                m_ref[g] = m_new
                acc_ref[g] = alpha * acc_ref[g] + _dot(vt, p.astype(BF16))
            return carry

        lax.fori_loop(0, na, online_tile, 0)

    for g in range(DSA_KV_HEADS):
        o_g = acc_ref[g] * (1.0 / l_ref[g])
        for hh in range(grp):
            h = g * grp + hh
            o_ref[:, h * HEAD_DIM:(h + 1) * HEAD_DIM] = o_g[:, hh * TQ:(hh + 1) * TQ].T.astype(o_ref.dtype)


def _dsa(bound, iqt, iw, dqt, ik, dk, vt, batch, seq, topk):
    TQ = DSA_TQ
    nq = seq // TQ
    t = batch * seq
    nh = DSA_HEADS * HEAD_DIM
    nkv = DSA_KV_HEADS * HEAD_DIM
    grp = DSA_HEADS // DSA_KV_HEADS
    col = lambda b, i: (0, b * nq + i)
    return pl.pallas_call(
        functools.partial(_dsa_body, topk),
        grid=(batch, nq),
        in_specs=[
            pl.BlockSpec(memory_space=pltpu.SMEM),
            pl.BlockSpec((IDX_HEADS * IDX_DIM, TQ), col),
            pl.BlockSpec((TQ, LANES), lambda b, i: (b * nq + i, 0)),
            pl.BlockSpec((nh, TQ), col),
            pl.BlockSpec((seq, IDX_DIM), lambda b, i: (b, 0), pipeline_mode=pl.Buffered(1)),
            pl.BlockSpec((seq, nkv), lambda b, i: (b, 0), pipeline_mode=pl.Buffered(1)),
            pl.BlockSpec((nkv, seq), lambda b, i: (0, b), pipeline_mode=pl.Buffered(1)),
        ],
        out_specs=pl.BlockSpec((TQ, nh), lambda b, i: (b * nq + i, 0)),
        out_shape=jax.ShapeDtypeStruct((t, nh), BF16),
        scratch_shapes=[
            pltpu.VMEM((seq, TQ), jnp.int32),
            pltpu.VMEM((seq, TQ), jnp.int16),
            pltpu.VMEM((seq, TQ), jnp.int16),
            pltpu.VMEM((LANES, TQ), F32),
            pltpu.VMEM((DSA_KV_HEADS, HEAD_DIM, grp * TQ), BF16),
            pltpu.VMEM((DSA_KV_HEADS, 1, grp * TQ), F32),
            pltpu.VMEM((DSA_KV_HEADS, 1, grp * TQ), F32),
            pltpu.VMEM((DSA_KV_HEADS, HEAD_DIM, grp * TQ), F32),
        ],
        compiler_params=_params("parallel", "arbitrary"),
        name="dsa",
    )(bound, iqt, iw, dqt, ik, dk, vt)


def _merge_body(og_ref, od_ref, wg_ref, wd_ref, gg_ref, gd_ref, bgg_ref, bgd_ref, o_ref):
    yg = _dot(og_ref[...], wg_ref[...])
    yd = _dot(od_ref[...], wd_ref[...])
    sg = 1.0 / (1.0 + jnp.exp(-(gg_ref[...] + bgg_ref[...])))
    sd = 1.0 / (1.0 + jnp.exp(-(gd_ref[...] + bgd_ref[...])))
    o_ref[...] = (sg * yg + sd * yd).astype(o_ref.dtype)


def _merge(o_gla, o_dsa, wg, wd, p_gate, b_gate, tm=1024, tn=512):
    t, dg = o_gla.shape
    dd = o_dsa.shape[1]
    d = wg.shape[1]
    nb = d // tn
    return pl.pallas_call(
        _merge_body,
        grid=(t // tm, nb),
        in_specs=[
            pl.BlockSpec((tm, dg), lambda i, j: (i, 0)),
            pl.BlockSpec((tm, dd), lambda i, j: (i, 0)),
            pl.BlockSpec((dg, tn), lambda i, j: (0, j)),
            pl.BlockSpec((dd, tn), lambda i, j: (0, j)),
            pl.BlockSpec((tm, tn), lambda i, j: (i, j)),
            pl.BlockSpec((tm, tn), lambda i, j: (i, nb + j)),
            pl.BlockSpec((1, tn), lambda i, j: (0, j)),
            pl.BlockSpec((1, tn), lambda i, j: (0, nb + j)),
        ],
        out_specs=pl.BlockSpec((tm, tn), lambda i, j: (i, j)),
        out_shape=jax.ShapeDtypeStruct((t, d), BF16),
        compiler_params=_params("parallel", "parallel"),
        name="merge",
    )(o_gla, o_dsa, wg, wd, p_gate, p_gate, b_gate, b_gate)


def _outproj_body(x_ref, m_ref, w_ref, o_ref):
    o_ref[...] = x_ref[...] + _dot(m_ref[...], w_ref[...])


def _outproj(x2, m, w_out, tm=1024, tn=1024):
    t, d = x2.shape
    tn = min(tn, d)
    return pl.pallas_call(
        _outproj_body,
        grid=(t // tm, d // tn),
        in_specs=[
            pl.BlockSpec((tm, tn), lambda i, j: (i, j)),
            pl.BlockSpec((tm, d), lambda i, j: (i, 0)),
            pl.BlockSpec((d, tn), lambda i, j: (0, j)),
        ],
        out_specs=pl.BlockSpec((tm, tn), lambda i, j: (i, j)),
        out_shape=jax.ShapeDtypeStruct((t, d), F32),
        compiler_params=_params("parallel", "parallel"),
        name="out_proj",
    )(x2, m, w_out)


def _mlp_body(x_ref, g_ref, w1_ref, w2_ref, o_ref, h_scr):
    @pl.when(pl.program_id(1) == 0)
    def _():
        x = x_ref[...]
        h_scr[...] = (x * lax.rsqrt(jnp.mean(x * x, axis=-1, keepdims=True) + NORM_EPS) * g_ref[...]).astype(BF16)
        o_ref[...] = x

    u = jnp.maximum(_dot(h_scr[...], w1_ref[...]), 0.0)
    o_ref[...] += _dot((u * u).astype(BF16), w2_ref[...])


def _mlp(x1, g, w1, w2, tm=512, tf=1024):
    t, d = x1.shape
    ff = w1.shape[1]
    return pl.pallas_call(
        _mlp_body,
        grid=(t // tm, ff // tf),
        in_specs=[
            pl.BlockSpec((tm, d), lambda i, f: (i, 0)),
            pl.BlockSpec((1, d), lambda i, f: (0, 0)),
            pl.BlockSpec((d, tf), lambda i, f: (0, f)),
            pl.BlockSpec((tf, d), lambda i, f: (f, 0)),
        ],
        out_specs=pl.BlockSpec((tm, d), lambda i, f: (i, 0)),
        out_shape=jax.ShapeDtypeStruct((t, d), F32),
        scratch_shapes=[pltpu.VMEM((tm, d), BF16)],
        compiler_params=_params("parallel", "arbitrary"),
        name="mlp",
    )(x1, g.reshape(1, d), w1, w2)


def _rope_tables(seq, q_g, k_g, ik_g):
    pos = jnp.arange(seq, dtype=jnp.int32).astype(F32)

    def cs(rot):
        half = rot // 2
        inv = ROPE_THETA ** (-jnp.arange(half, dtype=F32) * 2.0 / rot)
        ang = pos[:, None] * inv[None, :]
        return jnp.cos(ang), jnp.sin(ang)

    c, s = cs(HEAD_DIM)
    hh = HEAD_DIM // 2
    ka = jnp.concatenate([c, c], axis=1) * k_g[None, :]
    kb = jnp.concatenate([-s * k_g[None, hh:], s * k_g[None, :hh]], axis=1)
    qs = (HEAD_DIM ** -0.5) * LOG2E
    qa = (jnp.concatenate([c, c], axis=1) * q_g[None, :] * qs).T
    qb = (jnp.concatenate([-s * q_g[None, hh:], s * q_g[None, :hh]], axis=1) * qs).T

    ci, si = cs(IDX_ROPE_DIM)
    h2 = IDX_ROPE_DIM // 2
    one = jnp.ones((seq, IDX_DIM - IDX_ROPE_DIM), F32)
    zero = jnp.zeros((seq, IDX_DIM - IDX_ROPE_DIM), F32)
    zh = jnp.zeros((seq, h2), F32)
    ia = jnp.concatenate([ci, ci, one], axis=1) * ik_g[None, :]
    ib1 = jnp.concatenate([-si * ik_g[None, h2:2 * h2], zh, zero], axis=1)
    ib2 = jnp.concatenate([zh, si * ik_g[None, :h2], zero], axis=1)
    iqs = IDX_DIM ** -0.5
    iqa = (jnp.concatenate([ci, ci, one], axis=1) * iqs).T
    iqb = (jnp.concatenate([-si, si, zero], axis=1) * iqs).T
    return (ka, kb, ia, ib1, ib2), (qa, qb), (iqa, iqb)


def kernel(x, norm1_g, w_in, gla_wg2, gla_bg, gla_norm_g, w_proj_gla, q_norm_g, k_norm_g, idx_k_norm_g,
           w_proj_dsa, b_gate, w_out, norm2_g, w_ff1, w_ff2):
    batch, seq, d = x.shape
    depth = w_in.shape[0]
    topk = min(TOPK_MAX, seq // 4)
    nqk = GLA_HEADS * GLA_DK
    nv = GLA_HEADS * GLA_DV
    ndq = DSA_HEADS * HEAD_DIM
    nkv = DSA_KV_HEADS * HEAD_DIM
    niq = IDX_HEADS * IDX_DIM
    splits = (nqk, nqk, nv, nv, GLA_GATE_RANK, ndq, nkv, nkv, niq, IDX_DIM, IDX_HEADS, 2 * d)
    offs = [0]
    for n in splits:
        offs.append(offs[-1] + n)
    o_gq, _, _, _, o_glr, o_dq, o_dk, o_dv, o_iq, o_ik, o_iw, o_gate, _ = offs
    assert (2 * nqk + 2 * nv) % 1024 == 0 and seq % 512 == 0 and d % 512 == 0

    x2 = x.reshape(batch * seq, d)
    for l in range(depth):
        w_gla, w_gate, w_k, wt_dq, wt_iq, wt_v = _repack_w_in(
            w_in[l], (o_gq, o_glr, o_dq, o_dk, o_dv, o_iq, o_ik, o_iw, o_gate, offs[-1]))
        wg2p = jnp.concatenate(
            [gla_wg2[l], jnp.zeros((LANES - GLA_GATE_RANK, nqk), F32)], axis=0).astype(BF16)
        k_tabs, q_tabs, iq_tabs = _rope_tables(seq, q_norm_g[l], k_norm_g[l], idx_k_norm_g[l])

        h, ht = _rmsnorm_both(x2, norm1_g[l])
        p_gla = _mm(h, w_gla, F32, 1024, 1024, "proj_gla")
        p_gate = _mm(h, w_gate, F32, 1024, 1024, "proj_gate")
        dk, ik, glr, iw = _proj_k(h, w_k, k_tabs, seq)
        dqt = _proj_t("dq", wt_dq, ht, q_tabs, seq, BF16, 1024)
        iqt = _proj_t("iq", wt_iq, ht, iq_tabs, seq, BF16, 1024)
        vt = _proj_t("v", wt_v, ht, (), seq, BF16, nkv)

        b_cum, drop = _gla_decay(glr, wg2p, gla_bg[l].reshape(1, nqk))
        chunks_per_tile = batch * seq // GLA_CHUNK // drop.shape[0]
        mild = (drop[:, :chunks_per_tile, 0].reshape(-1) <= GLA_MAX_SPLIT_LOG2).astype(jnp.int32)
        o_gla = _gla(p_gla, b_cum, mild, gla_norm_g[l].reshape(1, GLA_DV), batch, seq)
        bound = (1.02 * HEAD_DIM * (HEAD_DIM ** -0.5) * LOG2E
                 * jnp.max(jnp.abs(q_norm_g[l])) * jnp.max(jnp.abs(k_norm_g[l]))).reshape(1, 1).astype(F32)
        o_dsa = _dsa(bound, iqt, iw, dqt, ik, dk, vt, batch, seq, topk)

        m = _merge(o_gla, o_dsa, w_proj_gla[l].astype(BF16), w_proj_dsa[l].astype(BF16), p_gate,
                   b_gate[l].reshape(1, 2 * d))
        x1 = _outproj(x2, m, w_out[l].astype(BF16))
        x2 = _mlp(x1, norm2_g[l], w_ff1[l].astype(BF16), w_ff2[l].astype(BF16))
    return x2.reshape(batch, seq, d)
```

```python
import functools
import math

import jax
import jax.numpy as jnp
from jax import lax
from jax.experimental import pallas as pl
from jax.experimental.pallas import tpu as pltpu

BF16 = jnp.bfloat16
F32 = jnp.float32

NORM_EPS = 1e-6
ROPE_THETA = 10000.0
GLA_HEADS = 4
GLA_DK = 256
GLA_DV = 512
GLA_GATE_RANK = 16
GLA_GATE_NORMALIZER = 16.0
DSA_HEADS = 16
DSA_KV_HEADS = 4
HEAD_DIM = 128
IDX_HEADS = 16
IDX_DIM = 128
IDX_ROPE_DIM = 64
TOPK_MAX = 256

LANES = 128
VMEM_LIMIT = 56 * 1024 * 1024
GLA_CHUNK = 128
GLA_SUB = 16
GLA_MAX_SPLIT_LOG2 = 100.0
DSA_TQ = 256
DSA_TK = 512
DSA_ATK = 512
DSA_COUNT_ROWS = 64
DSA_MIN_DENOM = 2.0 ** -80
INT_MIN = -(2 ** 31)
HALF_BIAS = 2 ** 15
NEG_BIG = -1e30
LOG2E = math.log2(math.e)


def _loop_by_pairs(n, body):
    def pair(jj, carry):
        body(2 * jj)
        body(2 * jj + 1)
        return carry

    lax.fori_loop(0, n // 2, pair, 0)

    @pl.when(n % 2 == 1)
    def _():
        body(n - 1)


def _params(*sem):
    return pltpu.CompilerParams(dimension_semantics=sem, vmem_limit_bytes=VMEM_LIMIT)


def _dot(a, b):
    return jnp.dot(a, b, preferred_element_type=F32)


def _rows_block(nrows, d, row0, step):
    return pl.BlockSpec((pl.Element(nrows), pl.Element(d)), lambda i: (pl.multiple_of(row0 + i * step, 8), 0))


def _wrows_body(to_natural, w_ref, o_ref):
    w = w_ref[...]
    o_ref[...] = (w.T if to_natural else w).astype(o_ref.dtype)


def _w_rows(wt, row0, nrows, to_natural, name, tr=512):
    d = wt.shape[1]
    tr = min(tr, nrows)
    out_spec = pl.BlockSpec((d, tr), lambda i: (0, i)) if to_natural else pl.BlockSpec((tr, d), lambda i: (i, 0))
    return pl.pallas_call(
        functools.partial(_wrows_body, to_natural),
        grid=(nrows // tr,),
        in_specs=[_rows_block(tr, d, row0, tr)],
        out_specs=out_spec,
        out_shape=jax.ShapeDtypeStruct((d, nrows) if to_natural else (nrows, d), BF16),
        compiler_params=_params("parallel"),
        name=name,
    )(wt)


def _wk_body(dk_ref, ik_ref, glr_ref, iw_ref, o_ref):
    nkv = dk_ref.shape[0]
    o_ref[:, :nkv] = dk_ref[...].T.astype(BF16)
    o_ref[:, nkv:nkv + IDX_DIM] = ik_ref[...].T.astype(BF16)

    def narrow(ref):
        pad = jnp.zeros((LANES - ref.shape[0], ref.shape[1]), F32)
        return jnp.concatenate([ref[...], pad], axis=0).T.astype(BF16)

    o_ref[:, nkv + IDX_DIM:nkv + IDX_DIM + LANES] = narrow(glr_ref)
    o_ref[:, nkv + IDX_DIM + LANES:] = narrow(iw_ref)


def _w_key_side(wt, o_dk, n_dk, o_ik, o_glr, o_iw):
    d = wt.shape[1]
    n_k = n_dk + IDX_DIM + 2 * LANES
    return pl.pallas_call(
        _wk_body,
        grid=(1,),
        in_specs=[_rows_block(n_dk, d, o_dk, 0), _rows_block(IDX_DIM, d, o_ik, 0),
                  _rows_block(GLA_GATE_RANK, d, o_glr, 0), _rows_block(IDX_HEADS, d, o_iw, 0)],
        out_specs=pl.BlockSpec((d, n_k), lambda i: (0, 0)),
        out_shape=jax.ShapeDtypeStruct((d, n_k), BF16),
        compiler_params=_params("arbitrary"),
        name="w_key_side",
    )(wt, wt, wt, wt)


def _rms_body(x_ref, g_ref, h_ref, ht_ref):
    x = x_ref[...]
    y = x * lax.rsqrt(jnp.mean(x * x, axis=-1, keepdims=True) + NORM_EPS) * g_ref[...]
    h_ref[...] = y.astype(BF16)
    ht_ref[...] = y.T.astype(BF16)


def _rmsnorm_both(x2, g, tm=256):
    t, d = x2.shape
    return pl.pallas_call(
        _rms_body,
        grid=(t // tm,),
        in_specs=[pl.BlockSpec((tm, d), lambda i: (i, 0)), pl.BlockSpec((1, d), lambda i: (0, 0))],
        out_specs=[pl.BlockSpec((tm, d), lambda i: (i, 0)), pl.BlockSpec((d, tm), lambda i: (0, i))],
        out_shape=[jax.ShapeDtypeStruct((t, d), BF16), jax.ShapeDtypeStruct((d, t), BF16)],
        compiler_params=_params("parallel"),
        name="rmsnorm_in",
    )(x2, g.reshape(1, d))


def _mm_body(a_ref, w_ref, o_ref):
    o_ref[...] = _dot(a_ref[...], w_ref[...]).astype(o_ref.dtype)


def _mm(a, w, out_dtype, tm, tn, name):
    m, k = a.shape
    n = w.shape[1]
    return pl.pallas_call(
        _mm_body,
        grid=(m // tm, n // tn),
        in_specs=[pl.BlockSpec((tm, k), lambda i, j: (i, 0)), pl.BlockSpec((k, tn), lambda i, j: (0, j))],
        out_specs=pl.BlockSpec((tm, tn), lambda i, j: (i, j)),
        out_shape=jax.ShapeDtypeStruct((m, n), out_dtype),
        compiler_params=_params("parallel", "parallel"),
        name=name,
    )(a, w)


def _projk_body(h_ref, w_ref, ka_ref, kb_ref, ia_ref, ib1_ref, ib2_ref, dk_ref, ik_ref, glr_ref, iw_ref):
    p = _dot(h_ref[...], w_ref[...])
    nkv = DSA_KV_HEADS
    for j in range(nkv):
        pj = p[:, j * HEAD_DIM:(j + 1) * HEAD_DIM]
        n = pj * lax.rsqrt(jnp.mean(pj * pj, axis=-1, keepdims=True) + NORM_EPS)
        out = n * ka_ref[...] + pltpu.roll(n, HEAD_DIM // 2, 1) * kb_ref[...]
        dk_ref[:, j * HEAD_DIM:(j + 1) * HEAD_DIM] = out.astype(BF16)
    c0 = nkv * HEAD_DIM
    pi = p[:, c0:c0 + IDX_DIM]
    n = pi * lax.rsqrt(jnp.mean(pi * pi, axis=-1, keepdims=True) + NORM_EPS)
    half = IDX_ROPE_DIM // 2
    out = n * ia_ref[...] + pltpu.roll(n, IDX_DIM - half, 1) * ib1_ref[...] + pltpu.roll(n, half, 1) * ib2_ref[...]
    ik_ref[...] = out.astype(BF16)
    c1 = c0 + IDX_DIM
    glr_ref[...] = p[:, c1:c1 + LANES]
    iw_ref[...] = p[:, c1 + LANES:] * (IDX_HEADS ** -0.5)


def _proj_k(h, w_k, tabs, seq, tm=512):
    t, d = h.shape
    nblk = seq // tm
    ncol = w_k.shape[1]
    tab_spec = pl.BlockSpec((tm, LANES), lambda i: (i % nblk, 0))
    return pl.pallas_call(
        _projk_body,
        grid=(t // tm,),
        in_specs=[pl.BlockSpec((tm, d), lambda i: (i, 0)), pl.BlockSpec((d, ncol), lambda i: (0, 0))] + [tab_spec] * 5,
        out_specs=[pl.BlockSpec((tm, DSA_KV_HEADS * HEAD_DIM), lambda i: (i, 0)),
                   pl.BlockSpec((tm, IDX_DIM), lambda i: (i, 0)),
                   pl.BlockSpec((tm, LANES), lambda i: (i, 0)),
                   pl.BlockSpec((tm, LANES), lambda i: (i, 0))],
        out_shape=[jax.ShapeDtypeStruct((t, DSA_KV_HEADS * HEAD_DIM), BF16),
                   jax.ShapeDtypeStruct((t, IDX_DIM), BF16),
                   jax.ShapeDtypeStruct((t, LANES), F32),
                   jax.ShapeDtypeStruct((t, LANES), F32)],
        compiler_params=_params("parallel"),
        name="proj_k",
    )(h, w_k, *tabs)


def _projt_body(kind, w_ref, ht_ref, *rest):
    p = _dot(w_ref[...], ht_ref[...])
    o_ref = rest[-1]
    rows = p.shape[0]
    if kind == "v":
        o_ref[...] = p.astype(o_ref.dtype)
        return
    ta_ref, tb_ref = rest[0], rest[1]
    for j in range(rows // HEAD_DIM):
        pj = p[j * HEAD_DIM:(j + 1) * HEAD_DIM]
        if kind == "dq":
            n = pj * lax.rsqrt(jnp.mean(pj * pj, axis=0, keepdims=True) + NORM_EPS)
            sw = jnp.concatenate([n[HEAD_DIM // 2:], n[:HEAD_DIM // 2]], axis=0)
        else:
            n = pj
            half = IDX_ROPE_DIM // 2
            sw = jnp.concatenate([n[half:2 * half], n[:half], n[2 * half:]], axis=0)
        o_ref[j * HEAD_DIM:(j + 1) * HEAD_DIM, :] = (n * ta_ref[...] + sw * tb_ref[...]).astype(o_ref.dtype)


def _proj_t(kind, w_t, ht, tabs, seq, out_dtype, tc, tt=512):
    c, d = w_t.shape
    t = ht.shape[1]
    nblk = seq // tt
    tab_spec = pl.BlockSpec((HEAD_DIM, tt), lambda i, j: (0, j % nblk))
    return pl.pallas_call(
        functools.partial(_projt_body, kind),
        grid=(c // tc, t // tt),
        in_specs=[pl.BlockSpec((tc, d), lambda i, j: (i, 0)), pl.BlockSpec((d, tt), lambda i, j: (0, j))]
        + [tab_spec] * len(tabs),
        out_specs=pl.BlockSpec((tc, tt), lambda i, j: (i, j)),
        out_shape=jax.ShapeDtypeStruct((c, t), out_dtype),
        compiler_params=_params("parallel", "parallel"),
        name="proj_t_" + kind,
    )(w_t, ht, *tabs)


def _gla_scores_explicit(q, k, b, bt):
    C, SB = GLA_CHUNK, GLA_SUB
    kt = k.T
    lane = lax.broadcasted_iota(jnp.int32, (SB, C), 1)
    rowi = lax.broadcasted_iota(jnp.int32, (SB, 1), 0)
    blocks = []
    for blk in range(C // SB):
        s = blk * SB
        q_i = q[s:s + SB]
        b_i = b[s:s + SB]
        a = jnp.zeros((SB, C), F32)
        for j in range(SB):
            r = s + j
            dec = jnp.where(rowi >= j, jnp.exp2(b_i - b[r:r + 1, :]), 0.0)
            col = jnp.sum(q_i * k[r:r + 1, :] * dec, axis=-1, keepdims=True)
            a = jnp.where(lane == r, col, a)
        if blk > 0:
            q_s = q_i * jnp.exp2(b_i - b[s - 1:s, :])
            k_s = kt * jnp.exp2(jnp.minimum(bt[:, s - 1:s] - bt, 0.0))
            a = jnp.where(lane < s, _dot(q_s.astype(BF16), k_s.astype(BF16)), a)
        blocks.append(a)
    return jnp.concatenate(blocks, axis=0)


def _gla_scores_split(q, k, b):
    C, SB = GLA_CHUNK, GLA_SUB
    nt = (((1,), (1,)), ((), ()))
    ri = lax.broadcasted_iota(jnp.int32, (C, C), 0)
    ci = lax.broadcasted_iota(jnp.int32, (C, C), 1)
    zeros = lambda n: jnp.zeros((n, GLA_DK), F32)
    a = jnp.zeros((C, C), F32)
    span = C // 2
    while span >= SB:
        q_parts, k_parts = [], []
        for lo in range(0, C, 2 * span):
            mid, hi = lo + span, lo + 2 * span
            ref = b[mid - 1:mid, :]
            k_parts += [k[lo:mid] * jnp.exp2(ref - b[lo:mid]), zeros(span)]
            q_parts += [zeros(span), q[mid:hi] * jnp.exp2(b[mid:hi] - ref)]
        s_l = lax.dot_general(jnp.concatenate(q_parts, axis=0).astype(BF16),
                              jnp.concatenate(k_parts, axis=0).astype(BF16), nt, preferred_element_type=F32)
        shift = (2 * span).bit_length() - 1
        a = a + jnp.where(jnp.right_shift(ri, shift) == jnp.right_shift(ci, shift), s_l, 0.0)
        span //= 2
    q_parts, k_parts = [], []
    for lo in range(0, C, SB):
        gap = b[lo:lo + SB] - b[lo - 1:lo, :] if lo > 0 else b[lo:lo + SB]
        q_parts.append(q[lo:lo + SB] * jnp.exp2(gap))
        k_parts.append(k[lo:lo + SB] * jnp.exp2(jnp.minimum(-gap, GLA_MAX_SPLIT_LOG2)))
    s_d = lax.dot_general(jnp.concatenate(q_parts, axis=0).astype(BF16),
                          jnp.concatenate(k_parts, axis=0).astype(BF16), nt, preferred_element_type=F32)
    shift = SB.bit_length() - 1
    same = jnp.right_shift(ri, shift) == jnp.right_shift(ci, shift)
    return a + jnp.where(same, jnp.where(ci <= ri, s_d, 0.0), 0.0)


def _gla_head(h, explicit, q_ref, k_ref, v_ref, gr_ref, ng_ref, o_ref, st_ref, b_ref):
    C = GLA_CHUNK
    qc = slice(h * GLA_DK, (h + 1) * GLA_DK)
    vc = slice(h * GLA_DV, (h + 1) * GLA_DV)
    q = q_ref[:, qc] * (GLA_DK ** -0.5)
    k = k_ref[:, qc]
    vb = v_ref[:, vc].astype(BF16)
    b = b_ref[:, qc]
    bt = b.T

    o = _dot((q * jnp.exp2(b)).astype(BF16), st_ref[h].astype(BF16))
    a = _gla_scores_explicit(q, k, b, bt) if explicit else _gla_scores_split(q, k, b)
    o = o + _dot(a.astype(BF16), vb)

    k_dec = k * jnp.exp2(b[C - 1:C, :] - b)
    st_ref[h] = st_ref[h] * jnp.exp2(bt[:, C - 1:C]) + _dot(k_dec.T.astype(BF16), vb)

    on = o * lax.rsqrt(jnp.mean(o * o, axis=-1, keepdims=True) + NORM_EPS) * ng_ref[...]
    gr = gr_ref[:, vc]
    o_ref[:, vc] = (on * gr * (1.0 / (1.0 + jnp.exp(-gr)))).astype(o_ref.dtype)


def _decay_body(glr_ref, wg2_ref, bg_ref, b_ref, drop_ref):
    C, SB = GLA_CHUNK, GLA_SUB
    z = _dot(glr_ref[...].astype(BF16), wg2_ref[...]) + bg_ref[...]
    g = (jnp.minimum(z, 0.0) - jnp.log1p(jnp.exp(-jnp.abs(z)))) * (LOG2E / GLA_GATE_NORMALIZER)
    ri = lax.broadcasted_iota(jnp.int32, (C, C), 0)
    ci = lax.broadcasted_iota(jnp.int32, (C, C), 1)
    tri = jnp.where(ci <= ri, 1.0, 0.0).astype(BF16)
    g1 = g.astype(BF16)
    r1 = g - g1.astype(F32)
    g2 = r1.astype(BF16)
    g3 = (r1 - g2.astype(F32)).astype(BF16)
    drop_ref[...] = jnp.zeros_like(drop_ref)
    for ch in range(glr_ref.shape[0] // C):
        rows = slice(ch * C, (ch + 1) * C)
        b = _dot(tri, g1[rows]) + _dot(tri, g2[rows]) + _dot(tri, g3[rows])
        b_ref[rows, :] = b
        drop = -b[SB - 1:SB, :]
        for blk in range(1, C // SB):
            s = blk * SB
            drop = jnp.maximum(drop, b[s - 1:s, :] - b[s + SB - 1:s + SB, :])
        drop_ref[0, ch:ch + 1, :] = jnp.broadcast_to(jnp.max(drop, axis=-1, keepdims=True), (1, LANES))


def _gla_decay(glr, wg2p, bg, tm=512):
    t = glr.shape[0]
    nqk = wg2p.shape[1]
    assert tm // GLA_CHUNK <= 8
    return pl.pallas_call(
        _decay_body,
        grid=(t // tm,),
        in_specs=[
            pl.BlockSpec((tm, LANES), lambda i: (i, 0)),
            pl.BlockSpec((LANES, nqk), lambda i: (0, 0)),
            pl.BlockSpec((1, nqk), lambda i: (0, 0)),
        ],
        out_specs=[pl.BlockSpec((tm, nqk), lambda i: (i, 0)), pl.BlockSpec((1, 8, LANES), lambda i: (i, 0, 0))],
        out_shape=[jax.ShapeDtypeStruct((t, nqk), F32), jax.ShapeDtypeStruct((t // tm, 8, LANES), F32)],
        compiler_params=_params("parallel"),
        name="gla_decay",
    )(glr, wg2p, bg)


def _gla_body(mild_ref, q_ref, k_ref, v_ref, gr_ref, b_ref, ng_ref, o_ref, st_ref):
    @pl.when(pl.program_id(1) == 0)
    def _():
        st_ref[...] = jnp.zeros_like(st_ref)

    refs = (q_ref, k_ref, v_ref, gr_ref, ng_ref, o_ref, st_ref, b_ref)
    mild = mild_ref[pl.program_id(0) * pl.num_programs(1) + pl.program_id(1)] == 1

    @pl.when(mild)
    def _():
        for h in range(GLA_HEADS):
            _gla_head(h, False, *refs)

    @pl.when(jnp.logical_not(mild))
    def _():
        for h in range(GLA_HEADS):
            _gla_head(h, True, *refs)


def _gla(p_gla, b_cum, mild, ng, batch, seq):
    C = GLA_CHUNK
    nc = seq // C
    t = batch * seq
    nqk = GLA_HEADS * GLA_DK
    nv = GLA_HEADS * GLA_DV
    vblk = 2 * nqk // nv
    row = lambda b, c: b * nc + c
    return pl.pallas_call(
        _gla_body,
        grid_spec=pltpu.PrefetchScalarGridSpec(
            num_scalar_prefetch=1,
            grid=(batch, nc),
            in_specs=[
                pl.BlockSpec((C, nqk), lambda b, c, m: (row(b, c), 0)),
                pl.BlockSpec((C, nqk), lambda b, c, m: (row(b, c), 1)),
                pl.BlockSpec((C, nv), lambda b, c, m: (row(b, c), vblk)),
                pl.BlockSpec((C, nv), lambda b, c, m: (row(b, c), vblk + 1)),
                pl.BlockSpec((C, nqk), lambda b, c, m: (row(b, c), 0)),
                pl.BlockSpec((1, GLA_DV), lambda b, c, m: (0, 0)),
            ],
            out_specs=pl.BlockSpec((C, nv), lambda b, c, m: (row(b, c), 0)),
            scratch_shapes=[pltpu.VMEM((GLA_HEADS, GLA_DK, GLA_DV), F32)],
        ),
        out_shape=jax.ShapeDtypeStruct((t, nv), BF16),
        compiler_params=_params("parallel", "arbitrary"),
        name="gla",
    )(mild, p_gla, p_gla, p_gla, p_gla, b_cum, ng)


def _dsa_body(topk, bound_ref, iqt_ref, iw_ref, dqt_ref, ik_ref, dk_ref, vt_ref, o_ref,
              sc_ref, hi_ref, lo_ref, iwt_ref, qg_ref, m_ref, l_ref, acc_ref):
    TQ, TK = DSA_TQ, DSA_TK
    i = pl.program_id(1)
    nk = (i * TQ + TQ + TK - 1) // TK
    q_pos = i * TQ + lax.broadcasted_iota(jnp.int32, (TK, TQ), 1)
    k_off = lax.broadcasted_iota(jnp.int32, (TK, TQ), 0)
    iwt_ref[...] = iw_ref[...].T

    def score_tile(j, carry):
        base = pl.multiple_of(j * TK, TK)
        ikt = ik_ref[pl.ds(base, TK), :]
        acc = jnp.zeros((TK, TQ), F32)
        for h in range(IDX_HEADS):
            r = _dot(ikt, iqt_ref[h * IDX_DIM:(h + 1) * IDX_DIM, :])
            acc = acc + jnp.maximum(r, 0.0) * iwt_ref[h:h + 1, :]
        bits = pltpu.bitcast(acc, jnp.int32)
        key = jnp.where(bits < 0, bits ^ jnp.int32(0x7FFFFFFF), bits)
        key = jnp.where(base + k_off <= q_pos, key, jnp.int32(INT_MIN))
        sc_ref[pl.ds(base, TK), :] = key
        hi_ref[pl.ds(base, TK), :] = jnp.right_shift(key, 16).astype(jnp.int16)
        lo_ref[pl.ds(base, TK), :] = ((key & jnp.int32(0xFFFF)) - jnp.int32(HALF_BIAS)).astype(jnp.int16)
        return carry

    _loop_by_pairs(nk, lambda j: score_tile(j, 0))

    CR = DSA_COUNT_ROWS

    def count_ge16(ref, cand16):
        def body(j, cnt):
            base = pl.multiple_of(j * TK, TK)
            hit = jnp.where(ref[pl.ds(base, TK), :] >= cand16, jnp.int16(1), jnp.int16(0))
            for r in range(TK // CR):
                cnt = cnt + hit[r * CR:(r + 1) * CR]
            return cnt
        cnt = lax.fori_loop(0, nk, body, jnp.zeros((CR, TQ), jnp.int16))
        return jnp.sum(cnt.astype(jnp.int32), axis=0, keepdims=True)

    def kth_largest16(ref, rank):
        zero = jnp.zeros((1, TQ), jnp.int32)
        c0 = count_ge16(ref, zero.astype(jnp.int16))
        ok0 = c0 >= rank
        state0 = (jnp.where(ok0, zero, jnp.int32(-HALF_BIAS)), jnp.where(ok0, c0, nk * TK))

        def bit_step(it, state):
            t, cnt = state
            cand = t | jnp.left_shift(jnp.int32(1), 14 - it)
            c = count_ge16(ref, cand.astype(jnp.int16))
            ok = c >= rank
            return jnp.where(ok, cand, t), jnp.where(ok, c, cnt)

        return lax.fori_loop(0, 15, bit_step, state0)

    def count_above16(ref, t):
        c = count_ge16(ref, jnp.minimum(t + 1, HALF_BIAS - 1).astype(jnp.int16))
        return jnp.where(t >= HALF_BIAS - 1, 0, c)

    t_hi, n_hi = kth_largest16(hi_ref, topk)
    above = count_above16(hi_ref, t_hi)
    t_hi16 = t_hi.astype(jnp.int16)

    def low_tile(j, carry):
        base = pl.multiple_of(j * TK, TK)
        rows = pl.ds(base, TK)
        lo_ref[rows, :] = jnp.where(hi_ref[rows, :] == t_hi16, lo_ref[rows, :], jnp.int16(-HALF_BIAS))
        return carry

    lax.fori_loop(0, nk, low_tile, 0)
    t_lo, n_lo = kth_largest16(lo_ref, topk - above)
    thr = jnp.left_shift(t_hi, 16) + (t_lo + HALF_BIAS)

    n_lo = jnp.where(t_lo == -HALF_BIAS, n_hi - above, n_lo)
    n_adm = i * TQ + 1 + lax.broadcasted_iota(jnp.int32, (1, TQ), 1)
    surplus = jnp.where(n_adm > topk, above + n_lo - topk, 0)

    @pl.when(jnp.max(surplus) > 0)
    def _():
        quota = jnp.where(surplus > 0, topk - above - count_above16(lo_ref, t_lo), nk * TK).astype(F32)
        before = (lax.broadcasted_iota(jnp.int32, (TK, TK), 1)
                  < lax.broadcasted_iota(jnp.int32, (TK, TK), 0))
        before = jnp.where(before, 1.0, 0.0).astype(BF16)

        def tie_tile(j, seen):
            base = pl.multiple_of(j * TK, TK)
            key = sc_ref[pl.ds(base, TK), :]
            tie = jnp.where(key == thr, 1.0, 0.0)
            rank = seen + _dot(before, tie.astype(BF16))
            sc_ref[pl.ds(base, TK), :] = jnp.where((tie > 0.0) & (rank >= quota), jnp.int32(INT_MIN), key)
            return seen + jnp.sum(tie, axis=0, keepdims=True)

        lax.fori_loop(0, nk, tie_tile, jnp.zeros((1, TQ), F32))

    thr = jnp.maximum(thr, jnp.int32(INT_MIN + 1))

    grp = DSA_HEADS // DSA_KV_HEADS
    AK = DSA_ATK
    na = (i * TQ + TQ + AK - 1) // AK
    for g in range(DSA_KV_HEADS):
        qg_ref[g] = jnp.concatenate(
            [dqt_ref[(g * grp + hh) * HEAD_DIM:(g * grp + hh + 1) * HEAD_DIM, :] for hh in range(grp)], axis=1)

    def tile_operands(j, g):
        base = pl.multiple_of(j * AK, AK)
        kt = dk_ref[pl.ds(base, AK), g * HEAD_DIM:(g + 1) * HEAD_DIM]
        vt = vt_ref[g * HEAD_DIM:(g + 1) * HEAD_DIM, pl.ds(base, AK)]
        return kt, vt

    def tile_bias(j, selected):
        base = pl.multiple_of(j * AK, AK)
        bias = jnp.where(sc_ref[pl.ds(base, AK), :] >= thr, selected, NEG_BIG)
        return jnp.concatenate([bias] * grp, axis=1)

    l_ref[...] = jnp.zeros_like(l_ref)
    acc_ref[...] = jnp.zeros_like(acc_ref)
    neg_bound = -bound_ref[0, 0]

    def bounded_tile(j, carry):
        bias = tile_bias(j, neg_bound)
        for g in range(DSA_KV_HEADS):
            kt, vt = tile_operands(j, g)
            p = jnp.exp2(_dot(kt, qg_ref[g]) + bias)
            l_ref[g] += jnp.sum(p, axis=0, keepdims=True)
            acc_ref[g] += _dot(vt, p.astype(BF16))
        return carry

    _loop_by_pairs(na, lambda j: bounded_tile(j, 0))

    @pl.when(jnp.logical_not(jnp.min(l_ref[...]) >= DSA_MIN_DENOM))
    def _():
        m_ref[...] = jnp.full_like(m_ref, NEG_BIG)
        l_ref[...] = jnp.zeros_like(l_ref)
        acc_ref[...] = jnp.zeros_like(acc_ref)

        def online_tile(j, carry):
            bias = tile_bias(j, 0.0)
            for g in range(DSA_KV_HEADS):
                kt, vt = tile_operands(j, g)
                s = _dot(kt, qg_ref[g]) + bias
                m_old = m_ref[g]
                m_new = jnp.maximum(m_old, jnp.max(s, axis=0, keepdims=True))
                p = jnp.exp2(s - m_new)
                alpha = jnp.exp2(m_old - m_new)
                l_ref[g] = alpha * l_ref[g] + jnp.sum(p, axis=0, keepdims=True)
                m_ref[g] = m_new
                acc_ref[g] = alpha * acc_ref[g] + _dot(vt, p.astype(BF16))
            return carry

        lax.fori_loop(0, na, online_tile, 0)

    for g in range(DSA_KV_HEADS):
        o_g = acc_ref[g] * (1.0 / l_ref[g])
        for hh in range(grp):
            h = g * grp + hh
            o_ref[:, h * HEAD_DIM:(h + 1) * HEAD_DIM] = o_g[:, hh * TQ:(hh + 1) * TQ].T.astype(o_ref.dtype)


def _dsa(bound, iqt, iw, dqt, ik, dk, vt, batch, seq, topk):
    TQ = DSA_TQ
    nq = seq // TQ
    t = batch * seq
    nh = DSA_HEADS * HEAD_DIM
    nkv = DSA_KV_HEADS * HEAD_DIM
    grp = DSA_HEADS // DSA_KV_HEADS
    col = lambda b, i: (0, b * nq + i)
    return pl.pallas_call(
        functools.partial(_dsa_body, topk),
        grid=(batch, nq),
        in_specs=[
            pl.BlockSpec(memory_space=pltpu.SMEM),
            pl.BlockSpec((IDX_HEADS * IDX_DIM, TQ), col),
            pl.BlockSpec((TQ, LANES), lambda b, i: (b * nq + i, 0)),
            pl.BlockSpec((nh, TQ), col),
            pl.BlockSpec((seq, IDX_DIM), lambda b, i: (b, 0), pipeline_mode=pl.Buffered(1)),
            pl.BlockSpec((seq, nkv), lambda b, i: (b, 0), pipeline_mode=pl.Buffered(1)),
            pl.BlockSpec((nkv, seq), lambda b, i: (0, b), pipeline_mode=pl.Buffered(1)),
        ],
        out_specs=pl.BlockSpec((TQ, nh), lambda b, i: (b * nq + i, 0)),
        out_shape=jax.ShapeDtypeStruct((t, nh), BF16),
        scratch_shapes=[
            pltpu.VMEM((seq, TQ), jnp.int32),
            pltpu.VMEM((seq, TQ), jnp.int16),
            pltpu.VMEM((seq, TQ), jnp.int16),
            pltpu.VMEM((LANES, TQ), F32),
            pltpu.VMEM((DSA_KV_HEADS, HEAD_DIM, grp * TQ), BF16),
            pltpu.VMEM((DSA_KV_HEADS, 1, grp * TQ), F32),
            pltpu.VMEM((DSA_KV_HEADS, 1, grp * TQ), F32),
            pltpu.VMEM((DSA_KV_HEADS, HEAD_DIM, grp * TQ), F32),
        ],
        compiler_params=_params("parallel", "arbitrary"),
        name="dsa",
    )(bound, iqt, iw, dqt, ik, dk, vt)


def _merge_body(og_ref, od_ref, wg_ref, wd_ref, gg_ref, gd_ref, bgg_ref, bgd_ref, o_ref):
    yg = _dot(og_ref[...], wg_ref[...])
    yd = _dot(od_ref[...], wd_ref[...])
    sg = 1.0 / (1.0 + jnp.exp(-(gg_ref[...] + bgg_ref[...])))
    sd = 1.0 / (1.0 + jnp.exp(-(gd_ref[...] + bgd_ref[...])))
    o_ref[...] = (sg * yg + sd * yd).astype(o_ref.dtype)


def _merge(o_gla, o_dsa, wg, wd, p_gate, b_gate, tm=1024, tn=512):
    t, dg = o_gla.shape
    dd = o_dsa.shape[1]
    d = wg.shape[1]
    nb = d // tn
    return pl.pallas_call(
        _merge_body,
        grid=(t // tm, nb),
        in_specs=[
            pl.BlockSpec((tm, dg), lambda i, j: (i, 0)),
            pl.BlockSpec((tm, dd), lambda i, j: (i, 0)),
            pl.BlockSpec((dg, tn), lambda i, j: (0, j)),
            pl.BlockSpec((dd, tn), lambda i, j: (0, j)),
            pl.BlockSpec((tm, tn), lambda i, j: (i, j)),
            pl.BlockSpec((tm, tn), lambda i, j: (i, nb + j)),
            pl.BlockSpec((1, tn), lambda i, j: (0, j)),
            pl.BlockSpec((1, tn), lambda i, j: (0, nb + j)),
        ],
        out_specs=pl.BlockSpec((tm, tn), lambda i, j: (i, j)),
        out_shape=jax.ShapeDtypeStruct((t, d), BF16),
        compiler_params=_params("parallel", "parallel"),
        name="merge",
    )(o_gla, o_dsa, wg, wd, p_gate, p_gate, b_gate, b_gate)


def _outproj_body(x_ref, m_ref, w_ref, o_ref):
    o_ref[...] = x_ref[...] + _dot(m_ref[...], w_ref[...])


def _outproj(x2, m, w_out, tm=1024, tn=1024):
    t, d = x2.shape
    tn = min(tn, d)
    return pl.pallas_call(
        _outproj_body,
        grid=(t // tm, d // tn),
        in_specs=[
            pl.BlockSpec((tm, tn), lambda i, j: (i, j)),
            pl.BlockSpec((tm, d), lambda i, j: (i, 0)),
            pl.BlockSpec((d, tn), lambda i, j: (0, j)),
        ],
        out_specs=pl.BlockSpec((tm, tn), lambda i, j: (i, j)),
        out_shape=jax.ShapeDtypeStruct((t, d), F32),
        compiler_params=_params("parallel", "parallel"),
        name="out_proj",
    )(x2, m, w_out)


def _mlp_body(x_ref, g_ref, w1_ref, w2_ref, o_ref, h_scr):
    @pl.when(pl.program_id(1) == 0)
    def _():
        x = x_ref[...]
        h_scr[...] = (x * lax.rsqrt(jnp.mean(x * x, axis=-1, keepdims=True) + NORM_EPS) * g_ref[...]).astype(BF16)
        o_ref[...] = x

    u = jnp.maximum(_dot(h_scr[...], w1_ref[...]), 0.0)
    o_ref[...] += _dot((u * u).astype(BF16), w2_ref[...])


def _mlp(x1, g, w1, w2, tm=512, tf=1024):
    t, d = x1.shape
    ff = w1.shape[1]
    return pl.pallas_call(
        _mlp_body,
        grid=(t // tm, ff // tf),
        in_specs=[
            pl.BlockSpec((tm, d), lambda i, f: (i, 0)),
            pl.BlockSpec((1, d), lambda i, f: (0, 0)),
            pl.BlockSpec((d, tf), lambda i, f: (0, f)),
            pl.BlockSpec((tf, d), lambda i, f: (f, 0)),
        ],
        out_specs=pl.BlockSpec((tm, d), lambda i, f: (i, 0)),
        out_shape=jax.ShapeDtypeStruct((t, d), F32),
        scratch_shapes=[pltpu.VMEM((tm, d), BF16)],
        compiler_params=_params("parallel", "arbitrary"),
        name="mlp",
    )(x1, g.reshape(1, d), w1, w2)


def _rope_tables(seq, q_g, k_g, ik_g):
    pos = jnp.arange(seq, dtype=jnp.int32).astype(F32)

    def cs(rot):
        half = rot // 2
        inv = ROPE_THETA ** (-jnp.arange(half, dtype=F32) * 2.0 / rot)
        ang = pos[:, None] * inv[None, :]
        return jnp.cos(ang), jnp.sin(ang)

    c, s = cs(HEAD_DIM)
    hh = HEAD_DIM // 2
    ka = jnp.concatenate([c, c], axis=1) * k_g[None, :]
    kb = jnp.concatenate([-s * k_g[None, hh:], s * k_g[None, :hh]], axis=1)
    qs = (HEAD_DIM ** -0.5) * LOG2E
    qa = (jnp.concatenate([c, c], axis=1) * q_g[None, :] * qs).T
    qb = (jnp.concatenate([-s * q_g[None, hh:], s * q_g[None, :hh]], axis=1) * qs).T

    ci, si = cs(IDX_ROPE_DIM)
    h2 = IDX_ROPE_DIM // 2
    one = jnp.ones((seq, IDX_DIM - IDX_ROPE_DIM), F32)
    zero = jnp.zeros((seq, IDX_DIM - IDX_ROPE_DIM), F32)
    zh = jnp.zeros((seq, h2), F32)
    ia = jnp.concatenate([ci, ci, one], axis=1) * ik_g[None, :]
    ib1 = jnp.concatenate([-si * ik_g[None, h2:2 * h2], zh, zero], axis=1)
    ib2 = jnp.concatenate([zh, si * ik_g[None, :h2], zero], axis=1)
    iqs = IDX_DIM ** -0.5
    iqa = (jnp.concatenate([ci, ci, one], axis=1) * iqs).T
    iqb = (jnp.concatenate([-si, si, zero], axis=1) * iqs).T
    return (ka, kb, ia, ib1, ib2), (qa, qb), (iqa, iqb)


def kernel(x, norm1_g, w_in, gla_wg2, gla_bg, gla_norm_g, w_proj_gla, q_norm_g, k_norm_g, idx_k_norm_g,
           w_proj_dsa, b_gate, w_out, norm2_g, w_ff1, w_ff2):
    batch, seq, d = x.shape
    depth = w_in.shape[0]
    topk = min(TOPK_MAX, seq // 4)
    nqk = GLA_HEADS * GLA_DK
    nv = GLA_HEADS * GLA_DV
    ndq = DSA_HEADS * HEAD_DIM
    nkv = DSA_KV_HEADS * HEAD_DIM
    niq = IDX_HEADS * IDX_DIM
    splits = (nqk, nqk, nv, nv, GLA_GATE_RANK, ndq, nkv, nkv, niq, IDX_DIM, IDX_HEADS, 2 * d)
    offs = [0]
    for n in splits:
        offs.append(offs[-1] + n)
    o_gq, _, _, _, o_glr, o_dq, o_dk, o_dv, o_iq, o_ik, o_iw, o_gate, _ = offs
    assert (2 * nqk + 2 * nv) % 1024 == 0 and seq % 512 == 0 and d % 512 == 0

    x2 = x.reshape(batch * seq, d)
    for l in range(depth):
        wt = jnp.transpose(w_in[l])
        w_gla = _w_rows(wt, o_gq, o_glr - o_gq, True, "w_gla")
        w_gate = _w_rows(wt, o_gate, offs[-1] - o_gate, True, "w_gate")
        w_k = _w_key_side(wt, o_dk, nkv, o_ik, o_glr, o_iw)
        wt_dq = _w_rows(wt, o_dq, ndq, False, "wt_dq")
        wt_iq = _w_rows(wt, o_iq, niq, False, "wt_iq")
        wt_v = _w_rows(wt, o_dv, nkv, False, "wt_v")
        wg2p = jnp.concatenate(
            [gla_wg2[l], jnp.zeros((LANES - GLA_GATE_RANK, nqk), F32)], axis=0).astype(BF16)
        k_tabs, q_tabs, iq_tabs = _rope_tables(seq, q_norm_g[l], k_norm_g[l], idx_k_norm_g[l])

        h, ht = _rmsnorm_both(x2, norm1_g[l])
        p_gla = _mm(h, w_gla, F32, 1024, 1024, "proj_gla")
        p_gate = _mm(h, w_gate, F32, 1024, 1024, "proj_gate")
        dk, ik, glr, iw = _proj_k(h, w_k, k_tabs, seq)
        dqt = _proj_t("dq", wt_dq, ht, q_tabs, seq, BF16, 1024)
        iqt = _proj_t("iq", wt_iq, ht, iq_tabs, seq, BF16, 1024)
        vt = _proj_t("v", wt_v, ht, (), seq, BF16, nkv)

        b_cum, drop = _gla_decay(glr, wg2p, gla_bg[l].reshape(1, nqk))
        chunks_per_tile = batch * seq // GLA_CHUNK // drop.shape[0]
        mild = (drop[:, :chunks_per_tile, 0].reshape(-1) <= GLA_MAX_SPLIT_LOG2).astype(jnp.int32)
        o_gla = _gla(p_gla, b_cum, mild, gla_norm_g[l].reshape(1, GLA_DV), batch, seq)
        bound = (1.02 * HEAD_DIM * (HEAD_DIM ** -0.5) * LOG2E
                 * jnp.max(jnp.abs(q_norm_g[l])) * jnp.max(jnp.abs(k_norm_g[l]))).reshape(1, 1).astype(F32)
        o_dsa = _dsa(bound, iqt, iw, dqt, ik, dk, vt, batch, seq, topk)

        m = _merge(o_gla, o_dsa, w_proj_gla[l].astype(BF16), w_proj_dsa[l].astype(BF16), p_gate,
                   b_gate[l].reshape(1, 2 * d))
        x1 = _outproj(x2, m, w_out[l].astype(BF16))
        x2 = _mlp(x1, norm2_g[l], w_ff1[l].astype(BF16), w_ff2[l].astype(BF16))
    return x2.reshape(batch, seq, d)
```

```python
import functools
import math

import jax
import jax.numpy as jnp
from jax import lax
from jax.experimental import pallas as pl
from jax.experimental.pallas import tpu as pltpu

BF16 = jnp.bfloat16
F32 = jnp.float32

NORM_EPS = 1e-6
ROPE_THETA = 10000.0
GLA_HEADS = 4
GLA_DK = 256
GLA_DV = 512
GLA_GATE_RANK = 16
GLA_GATE_NORMALIZER = 16.0
DSA_HEADS = 16
DSA_KV_HEADS = 4
HEAD_DIM = 128
IDX_HEADS = 16
IDX_DIM = 128
IDX_ROPE_DIM = 64
TOPK_MAX = 256

LANES = 128
VMEM_LIMIT = 56 * 1024 * 1024
GLA_CHUNK = 128
GLA_SUB = 16
GLA_MAX_SPLIT_LOG2 = 100.0
DSA_TQ = 256
DSA_TK = 512
DSA_ATK = 512
DSA_COUNT_ROWS = 64
DSA_MIN_DENOM = 2.0 ** -80
INT_MIN = -(2 ** 31)
HALF_BIAS = 2 ** 15
NEG_BIG = -1e30
LOG2E = math.log2(math.e)


def _loop_by_pairs(n, body):
    def pair(jj, carry):
        body(2 * jj)
        body(2 * jj + 1)
        return carry

    lax.fori_loop(0, n // 2, pair, 0)

    @pl.when(n % 2 == 1)
    def _():
        body(n - 1)


def _params(*sem):
    return pltpu.CompilerParams(dimension_semantics=sem, vmem_limit_bytes=VMEM_LIMIT)


def _dot(a, b):
    return jnp.dot(a, b, preferred_element_type=F32)


def _rows_block(nrows, d, row0, step):
    return pl.BlockSpec((pl.Element(nrows), pl.Element(d)), lambda i: (pl.multiple_of(row0 + i * step, 8), 0))


def _wrows_body(to_natural, w_ref, o_ref):
    w = w_ref[...]
    o_ref[...] = (w.T if to_natural else w).astype(o_ref.dtype)


def _w_rows(wt, row0, nrows, to_natural, name, tr=512):
    d = wt.shape[1]
    tr = min(tr, nrows)
    out_spec = pl.BlockSpec((d, tr), lambda i: (0, i)) if to_natural else pl.BlockSpec((tr, d), lambda i: (i, 0))
    return pl.pallas_call(
        functools.partial(_wrows_body, to_natural),
        grid=(nrows // tr,),
        in_specs=[_rows_block(tr, d, row0, tr)],
        out_specs=out_spec,
        out_shape=jax.ShapeDtypeStruct((d, nrows) if to_natural else (nrows, d), BF16),
        compiler_params=_params("parallel"),
        name=name,
    )(wt)


def _wk_body(dk_ref, ik_ref, glr_ref, iw_ref, o_ref):
    nkv = dk_ref.shape[0]
    o_ref[:, :nkv] = dk_ref[...].T.astype(BF16)
    o_ref[:, nkv:nkv + IDX_DIM] = ik_ref[...].T.astype(BF16)

    def narrow(ref):
        pad = jnp.zeros((LANES - ref.shape[0], ref.shape[1]), F32)
        return jnp.concatenate([ref[...], pad], axis=0).T.astype(BF16)

    o_ref[:, nkv + IDX_DIM:nkv + IDX_DIM + LANES] = narrow(glr_ref)
    o_ref[:, nkv + IDX_DIM + LANES:] = narrow(iw_ref)


def _w_key_side(wt, o_dk, n_dk, o_ik, o_glr, o_iw):
    d = wt.shape[1]
    n_k = n_dk + IDX_DIM + 2 * LANES
    return pl.pallas_call(
        _wk_body,
        grid=(1,),
        in_specs=[_rows_block(n_dk, d, o_dk, 0), _rows_block(IDX_DIM, d, o_ik, 0),
                  _rows_block(GLA_GATE_RANK, d, o_glr, 0), _rows_block(IDX_HEADS, d, o_iw, 0)],
        out_specs=pl.BlockSpec((d, n_k), lambda i: (0, 0)),
        out_shape=jax.ShapeDtypeStruct((d, n_k), BF16),
        compiler_params=_params("arbitrary"),
        name="w_key_side",
    )(wt, wt, wt, wt)


def _rms_body(x_ref, g_ref, h_ref, ht_ref):
    x = x_ref[...]
    y = x * lax.rsqrt(jnp.mean(x * x, axis=-1, keepdims=True) + NORM_EPS) * g_ref[...]
    h_ref[...] = y.astype(BF16)
    ht_ref[...] = y.T.astype(BF16)


def _rmsnorm_both(x2, g, tm=256):
    t, d = x2.shape
    return pl.pallas_call(
        _rms_body,
        grid=(t // tm,),
        in_specs=[pl.BlockSpec((tm, d), lambda i: (i, 0)), pl.BlockSpec((1, d), lambda i: (0, 0))],
        out_specs=[pl.BlockSpec((tm, d), lambda i: (i, 0)), pl.BlockSpec((d, tm), lambda i: (0, i))],
        out_shape=[jax.ShapeDtypeStruct((t, d), BF16), jax.ShapeDtypeStruct((d, t), BF16)],
        compiler_params=_params("parallel"),
        name="rmsnorm_in",
    )(x2, g.reshape(1, d))


def _mm_body(a_ref, w_ref, o_ref):
    o_ref[...] = _dot(a_ref[...], w_ref[...]).astype(o_ref.dtype)


def _mm(a, w, out_dtype, tm, tn, name):
    m, k = a.shape
    n = w.shape[1]
    return pl.pallas_call(
        _mm_body,
        grid=(m // tm, n // tn),
        in_specs=[pl.BlockSpec((tm, k), lambda i, j: (i, 0)), pl.BlockSpec((k, tn), lambda i, j: (0, j))],
        out_specs=pl.BlockSpec((tm, tn), lambda i, j: (i, j)),
        out_shape=jax.ShapeDtypeStruct((m, n), out_dtype),
        compiler_params=_params("parallel", "parallel"),
        name=name,
    )(a, w)


def _projk_body(h_ref, w_ref, ka_ref, kb_ref, ia_ref, ib1_ref, ib2_ref, dk_ref, ik_ref, glr_ref, iw_ref):
    p = _dot(h_ref[...], w_ref[...])
    nkv = DSA_KV_HEADS
    for j in range(nkv):
        pj = p[:, j * HEAD_DIM:(j + 1) * HEAD_DIM]
        n = pj * lax.rsqrt(jnp.mean(pj * pj, axis=-1, keepdims=True) + NORM_EPS)
        out = n * ka_ref[...] + pltpu.roll(n, HEAD_DIM // 2, 1) * kb_ref[...]
        dk_ref[:, j * HEAD_DIM:(j + 1) * HEAD_DIM] = out.astype(BF16)
    c0 = nkv * HEAD_DIM
    pi = p[:, c0:c0 + IDX_DIM]
    n = pi * lax.rsqrt(jnp.mean(pi * pi, axis=-1, keepdims=True) + NORM_EPS)
    half = IDX_ROPE_DIM // 2
    out = n * ia_ref[...] + pltpu.roll(n, IDX_DIM - half, 1) * ib1_ref[...] + pltpu.roll(n, half, 1) * ib2_ref[...]
    ik_ref[...] = out.astype(BF16)
    c1 = c0 + IDX_DIM
    glr_ref[...] = p[:, c1:c1 + LANES]
    iw_ref[...] = p[:, c1 + LANES:] * (IDX_HEADS ** -0.5)


def _proj_k(h, w_k, tabs, seq, tm=512):
    t, d = h.shape
    nblk = seq // tm
    ncol = w_k.shape[1]
    tab_spec = pl.BlockSpec((tm, LANES), lambda i: (i % nblk, 0))
    return pl.pallas_call(
        _projk_body,
        grid=(t // tm,),
        in_specs=[pl.BlockSpec((tm, d), lambda i: (i, 0)), pl.BlockSpec((d, ncol), lambda i: (0, 0))] + [tab_spec] * 5,
        out_specs=[pl.BlockSpec((tm, DSA_KV_HEADS * HEAD_DIM), lambda i: (i, 0)),
                   pl.BlockSpec((tm, IDX_DIM), lambda i: (i, 0)),
                   pl.BlockSpec((tm, LANES), lambda i: (i, 0)),
                   pl.BlockSpec((tm, LANES), lambda i: (i, 0))],
        out_shape=[jax.ShapeDtypeStruct((t, DSA_KV_HEADS * HEAD_DIM), BF16),
                   jax.ShapeDtypeStruct((t, IDX_DIM), BF16),
                   jax.ShapeDtypeStruct((t, LANES), F32),
                   jax.ShapeDtypeStruct((t, LANES), F32)],
        compiler_params=_params("parallel"),
        name="proj_k",
    )(h, w_k, *tabs)


def _projt_body(kind, w_ref, ht_ref, *rest):
    p = _dot(w_ref[...], ht_ref[...])
    o_ref = rest[-1]
    rows = p.shape[0]
    if kind == "v":
        o_ref[...] = p.astype(o_ref.dtype)
        return
    ta_ref, tb_ref = rest[0], rest[1]
    for j in range(rows // HEAD_DIM):
        pj = p[j * HEAD_DIM:(j + 1) * HEAD_DIM]
        if kind == "dq":
            n = pj * lax.rsqrt(jnp.mean(pj * pj, axis=0, keepdims=True) + NORM_EPS)
            sw = jnp.concatenate([n[HEAD_DIM // 2:], n[:HEAD_DIM // 2]], axis=0)
        else:
            n = pj
            half = IDX_ROPE_DIM // 2
            sw = jnp.concatenate([n[half:2 * half], n[:half], n[2 * half:]], axis=0)
        o_ref[j * HEAD_DIM:(j + 1) * HEAD_DIM, :] = (n * ta_ref[...] + sw * tb_ref[...]).astype(o_ref.dtype)


def _proj_t(kind, w_t, ht, tabs, seq, out_dtype, tc, tt=512):
    c, d = w_t.shape
    t = ht.shape[1]
    nblk = seq // tt
    tab_spec = pl.BlockSpec((HEAD_DIM, tt), lambda i, j: (0, j % nblk))
    return pl.pallas_call(
        functools.partial(_projt_body, kind),
        grid=(c // tc, t // tt),
        in_specs=[pl.BlockSpec((tc, d), lambda i, j: (i, 0)), pl.BlockSpec((d, tt), lambda i, j: (0, j))]
        + [tab_spec] * len(tabs),
        out_specs=pl.BlockSpec((tc, tt), lambda i, j: (i, j)),
        out_shape=jax.ShapeDtypeStruct((c, t), out_dtype),
        compiler_params=_params("parallel", "parallel"),
        name="proj_t_" + kind,
    )(w_t, ht, *tabs)


def _gla_scores_explicit(q, k, b, bt):
    C, SB = GLA_CHUNK, GLA_SUB
    kt = k.T
    lane = lax.broadcasted_iota(jnp.int32, (SB, C), 1)
    rowi = lax.broadcasted_iota(jnp.int32, (SB, 1), 0)
    blocks = []
    for blk in range(C // SB):
        s = blk * SB
        q_i = q[s:s + SB]
        b_i = b[s:s + SB]
        a = jnp.zeros((SB, C), F32)
        for j in range(SB):
            r = s + j
            dec = jnp.where(rowi >= j, jnp.exp2(b_i - b[r:r + 1, :]), 0.0)
            col = jnp.sum(q_i * k[r:r + 1, :] * dec, axis=-1, keepdims=True)
            a = jnp.where(lane == r, col, a)
        if blk > 0:
            q_s = q_i * jnp.exp2(b_i - b[s - 1:s, :])
            k_s = kt * jnp.exp2(jnp.minimum(bt[:, s - 1:s] - bt, 0.0))
            a = jnp.where(lane < s, _dot(q_s.astype(BF16), k_s.astype(BF16)), a)
        blocks.append(a)
    return jnp.concatenate(blocks, axis=0)


def _gla_scores_split(q, k, b):
    C, SB = GLA_CHUNK, GLA_SUB
    nt = (((1,), (1,)), ((), ()))
    ri = lax.broadcasted_iota(jnp.int32, (C, C), 0)
    ci = lax.broadcasted_iota(jnp.int32, (C, C), 1)
    zeros = lambda n: jnp.zeros((n, GLA_DK), F32)
    a = jnp.zeros((C, C), F32)
    span = C // 2
    while span >= SB:
        q_parts, k_parts = [], []
        for lo in range(0, C, 2 * span):
            mid, hi = lo + span, lo + 2 * span
            ref = b[mid - 1:mid, :]
            k_parts += [k[lo:mid] * jnp.exp2(ref - b[lo:mid]), zeros(span)]
            q_parts += [zeros(span), q[mid:hi] * jnp.exp2(b[mid:hi] - ref)]
        s_l = lax.dot_general(jnp.concatenate(q_parts, axis=0).astype(BF16),
                              jnp.concatenate(k_parts, axis=0).astype(BF16), nt, preferred_element_type=F32)
        shift = (2 * span).bit_length() - 1
        a = a + jnp.where(jnp.right_shift(ri, shift) == jnp.right_shift(ci, shift), s_l, 0.0)
        span //= 2
    q_parts, k_parts = [], []
    for lo in range(0, C, SB):
        gap = b[lo:lo + SB] - b[lo - 1:lo, :] if lo > 0 else b[lo:lo + SB]
        q_parts.append(q[lo:lo + SB] * jnp.exp2(gap))
        k_parts.append(k[lo:lo + SB] * jnp.exp2(jnp.minimum(-gap, GLA_MAX_SPLIT_LOG2)))
    s_d = lax.dot_general(jnp.concatenate(q_parts, axis=0).astype(BF16),
                          jnp.concatenate(k_parts, axis=0).astype(BF16), nt, preferred_element_type=F32)
    shift = SB.bit_length() - 1
    same = jnp.right_shift(ri, shift) == jnp.right_shift(ci, shift)
    return a + jnp.where(same, jnp.where(ci <= ri, s_d, 0.0), 0.0)


def _gla_head(h, explicit, q_ref, k_ref, v_ref, gr_ref, ng_ref, o_ref, st_ref, b_ref):
    C = GLA_CHUNK
    qc = slice(h * GLA_DK, (h + 1) * GLA_DK)
    vc = slice(h * GLA_DV, (h + 1) * GLA_DV)
    q = q_ref[:, qc] * (GLA_DK ** -0.5)
    k = k_ref[:, qc]
    vb = v_ref[:, vc].astype(BF16)
    b = b_ref[:, qc]
    bt = b.T

    o = _dot((q * jnp.exp2(b)).astype(BF16), st_ref[h].astype(BF16))
    a = _gla_scores_explicit(q, k, b, bt) if explicit else _gla_scores_split(q, k, b)
    o = o + _dot(a.astype(BF16), vb)

    k_dec = k * jnp.exp2(b[C - 1:C, :] - b)
    st_ref[h] = st_ref[h] * jnp.exp2(bt[:, C - 1:C]) + _dot(k_dec.T.astype(BF16), vb)

    on = o * lax.rsqrt(jnp.mean(o * o, axis=-1, keepdims=True) + NORM_EPS) * ng_ref[...]
    gr = gr_ref[:, vc]
    o_ref[:, vc] = (on * gr * (1.0 / (1.0 + jnp.exp(-gr)))).astype(o_ref.dtype)


def _decay_body(glr_ref, wg2_ref, bg_ref, b_ref, drop_ref):
    C, SB = GLA_CHUNK, GLA_SUB
    z = _dot(glr_ref[...].astype(BF16), wg2_ref[...]) + bg_ref[...]
    g = (jnp.minimum(z, 0.0) - jnp.log1p(jnp.exp(-jnp.abs(z)))) * (LOG2E / GLA_GATE_NORMALIZER)
    ri = lax.broadcasted_iota(jnp.int32, (C, C), 0)
    ci = lax.broadcasted_iota(jnp.int32, (C, C), 1)
    tri = jnp.where(ci <= ri, 1.0, 0.0).astype(BF16)
    g1 = g.astype(BF16)
    r1 = g - g1.astype(F32)
    g2 = r1.astype(BF16)
    g3 = (r1 - g2.astype(F32)).astype(BF16)
    drop_ref[...] = jnp.zeros_like(drop_ref)
    for ch in range(glr_ref.shape[0] // C):
        rows = slice(ch * C, (ch + 1) * C)
        b = _dot(tri, g1[rows]) + _dot(tri, g2[rows]) + _dot(tri, g3[rows])
        b_ref[rows, :] = b
        drop = -b[SB - 1:SB, :]
        for blk in range(1, C // SB):
            s = blk * SB
            drop = jnp.maximum(drop, b[s - 1:s, :] - b[s + SB - 1:s + SB, :])
        drop_ref[0, ch:ch + 1, :] = jnp.broadcast_to(jnp.max(drop, axis=-1, keepdims=True), (1, LANES))


def _gla_decay(glr, wg2p, bg, tm=512):
    t = glr.shape[0]
    nqk = wg2p.shape[1]
    assert tm // GLA_CHUNK <= 8
    return pl.pallas_call(
        _decay_body,
        grid=(t // tm,),
        in_specs=[
            pl.BlockSpec((tm, LANES), lambda i: (i, 0)),
            pl.BlockSpec((LANES, nqk), lambda i: (0, 0)),
            pl.BlockSpec((1, nqk), lambda i: (0, 0)),
        ],
        out_specs=[pl.BlockSpec((tm, nqk), lambda i: (i, 0)), pl.BlockSpec((1, 8, LANES), lambda i: (i, 0, 0))],
        out_shape=[jax.ShapeDtypeStruct((t, nqk), F32), jax.ShapeDtypeStruct((t // tm, 8, LANES), F32)],
        compiler_params=_params("parallel"),
        name="gla_decay",
    )(glr, wg2p, bg)


def _gla_body(mild_ref, q_ref, k_ref, v_ref, gr_ref, b_ref, ng_ref, o_ref, st_ref):
    @pl.when(pl.program_id(1) == 0)
    def _():
        st_ref[...] = jnp.zeros_like(st_ref)

    refs = (q_ref, k_ref, v_ref, gr_ref, ng_ref, o_ref, st_ref, b_ref)
    mild = mild_ref[pl.program_id(0) * pl.num_programs(1) + pl.program_id(1)] == 1

    @pl.when(mild)
    def _():
        for h in range(GLA_HEADS):
            _gla_head(h, False, *refs)

    @pl.when(jnp.logical_not(mild))
    def _():
        for h in range(GLA_HEADS):
            _gla_head(h, True, *refs)


def _gla(p_gla, b_cum, mild, ng, batch, seq):
    C = GLA_CHUNK
    nc = seq // C
    t = batch * seq
    nqk = GLA_HEADS * GLA_DK
    nv = GLA_HEADS * GLA_DV
    vblk = 2 * nqk // nv
    row = lambda b, c: b * nc + c
    return pl.pallas_call(
        _gla_body,
        grid_spec=pltpu.PrefetchScalarGridSpec(
            num_scalar_prefetch=1,
            grid=(batch, nc),
            in_specs=[
                pl.BlockSpec((C, nqk), lambda b, c, m: (row(b, c), 0)),
                pl.BlockSpec((C, nqk), lambda b, c, m: (row(b, c), 1)),
                pl.BlockSpec((C, nv), lambda b, c, m: (row(b, c), vblk)),
                pl.BlockSpec((C, nv), lambda b, c, m: (row(b, c), vblk + 1)),
                pl.BlockSpec((C, nqk), lambda b, c, m: (row(b, c), 0)),
                pl.BlockSpec((1, GLA_DV), lambda b, c, m: (0, 0)),
            ],
            out_specs=pl.BlockSpec((C, nv), lambda b, c, m: (row(b, c), 0)),
            scratch_shapes=[pltpu.VMEM((GLA_HEADS, GLA_DK, GLA_DV), F32)],
        ),
        out_shape=jax.ShapeDtypeStruct((t, nv), BF16),
        compiler_params=_params("parallel", "arbitrary"),
        name="gla",
    )(mild, p_gla, p_gla, p_gla, p_gla, b_cum, ng)


def _dsa_body(topk, bound_ref, iqt_ref, iw_ref, dqt_ref, ik_ref, dk_ref, vt_ref, o_ref,
              sc_ref, hi_ref, lo_ref, iwt_ref, qg_ref, m_ref, l_ref, acc_ref):
    TQ, TK = DSA_TQ, DSA_TK
    i = pl.program_id(1)
    nk = (i * TQ + TQ + TK - 1) // TK
    q_pos = i * TQ + lax.broadcasted_iota(jnp.int32, (TK, TQ), 1)
    k_off = lax.broadcasted_iota(jnp.int32, (TK, TQ), 0)
    iwt_ref[...] = iw_ref[...].T

    def score_tile(j, carry):
        base = pl.multiple_of(j * TK, TK)
        ikt = ik_ref[pl.ds(base, TK), :]
        acc = jnp.zeros((TK, TQ), F32)
        for h in range(IDX_HEADS):
            r = _dot(ikt, iqt_ref[h * IDX_DIM:(h + 1) * IDX_DIM, :])
            acc = acc + jnp.maximum(r, 0.0) * iwt_ref[h:h + 1, :]
        bits = pltpu.bitcast(acc, jnp.int32)
        key = jnp.where(bits < 0, bits ^ jnp.int32(0x7FFFFFFF), bits)
        key = jnp.where(base + k_off <= q_pos, key, jnp.int32(INT_MIN))
        sc_ref[pl.ds(base, TK), :] = key
        hi_ref[pl.ds(base, TK), :] = jnp.right_shift(key, 16).astype(jnp.int16)
        lo_ref[pl.ds(base, TK), :] = ((key & jnp.int32(0xFFFF)) - jnp.int32(HALF_BIAS)).astype(jnp.int16)
        return carry

    _loop_by_pairs(nk, lambda j: score_tile(j, 0))

    CR = DSA_COUNT_ROWS

    def count_ge16(ref, cand16):
        def body(j, cnt):
            base = pl.multiple_of(j * TK, TK)
            hit = jnp.where(ref[pl.ds(base, TK), :] >= cand16, jnp.int16(1), jnp.int16(0))
            for r in range(TK // CR):
                cnt = cnt + hit[r * CR:(r + 1) * CR]
            return cnt
        cnt = lax.fori_loop(0, nk, body, jnp.zeros((CR, TQ), jnp.int16))
        return jnp.sum(cnt.astype(jnp.int32), axis=0, keepdims=True)

    def kth_largest16(ref, rank):
        zero = jnp.zeros((1, TQ), jnp.int32)
        c0 = count_ge16(ref, zero.astype(jnp.int16))
        ok0 = c0 >= rank
        state0 = (jnp.where(ok0, zero, jnp.int32(-HALF_BIAS)), jnp.where(ok0, c0, nk * TK))

        def bit_step(it, state):
            t, cnt = state
            cand = t | jnp.left_shift(jnp.int32(1), 14 - it)
            c = count_ge16(ref, cand.astype(jnp.int16))
            ok = c >= rank
            return jnp.where(ok, cand, t), jnp.where(ok, c, cnt)

        return lax.fori_loop(0, 15, bit_step, state0)

    def count_above16(ref, t):
        c = count_ge16(ref, jnp.minimum(t + 1, HALF_BIAS - 1).astype(jnp.int16))
        return jnp.where(t >= HALF_BIAS - 1, 0, c)

    t_hi, n_hi = kth_largest16(hi_ref, topk)
    above = count_above16(hi_ref, t_hi)
    t_hi16 = t_hi.astype(jnp.int16)

    def low_tile(j, carry):
        base = pl.multiple_of(j * TK, TK)
        rows = pl.ds(base, TK)
        lo_ref[rows, :] = jnp.where(hi_ref[rows, :] == t_hi16, lo_ref[rows, :], jnp.int16(-HALF_BIAS))
        return carry

    lax.fori_loop(0, nk, low_tile, 0)
    t_lo, n_lo = kth_largest16(lo_ref, topk - above)
    thr = jnp.left_shift(t_hi, 16) + (t_lo + HALF_BIAS)

    n_lo = jnp.where(t_lo == -HALF_BIAS, n_hi - above, n_lo)
    n_adm = i * TQ + 1 + lax.broadcasted_iota(jnp.int32, (1, TQ), 1)
    surplus = jnp.where(n_adm > topk, above + n_lo - topk, 0)

    @pl.when(jnp.max(surplus) > 0)
    def _():
        quota = jnp.where(surplus > 0, topk - above - count_above16(lo_ref, t_lo), nk * TK).astype(F32)
        before = (lax.broadcasted_iota(jnp.int32, (TK, TK), 1)
                  < lax.broadcasted_iota(jnp.int32, (TK, TK), 0))
        before = jnp.where(before, 1.0, 0.0).astype(BF16)

        def tie_tile(j, seen):
            base = pl.multiple_of(j * TK, TK)
            key = sc_ref[pl.ds(base, TK), :]
            tie = jnp.where(key == thr, 1.0, 0.0)
            rank = seen + _dot(before, tie.astype(BF16))
            sc_ref[pl.ds(base, TK), :] = jnp.where((tie > 0.0) & (rank >= quota), jnp.int32(INT_MIN), key)
            return seen + jnp.sum(tie, axis=0, keepdims=True)

        lax.fori_loop(0, nk, tie_tile, jnp.zeros((1, TQ), F32))

    thr = jnp.maximum(thr, jnp.int32(INT_MIN + 1))

    grp = DSA_HEADS // DSA_KV_HEADS
    AK = DSA_ATK
    na = (i * TQ + TQ + AK - 1) // AK
    for g in range(DSA_KV_HEADS):
        qg_ref[g] = jnp.concatenate(
            [dqt_ref[(g * grp + hh) * HEAD_DIM:(g * grp + hh + 1) * HEAD_DIM, :] for hh in range(grp)], axis=1)

    def tile_operands(j, g):
        base = pl.multiple_of(j * AK, AK)
        kt = dk_ref[pl.ds(base, AK), g * HEAD_DIM:(g + 1) * HEAD_DIM]
        vt = vt_ref[g * HEAD_DIM:(g + 1) * HEAD_DIM, pl.ds(base, AK)]
        return kt, vt

    def tile_bias(j, selected):
        base = pl.multiple_of(j * AK, AK)
        bias = jnp.where(sc_ref[pl.ds(base, AK), :] >= thr, selected, NEG_BIG)
        return jnp.concatenate([bias] * grp, axis=1)

    l_ref[...] = jnp.zeros_like(l_ref)
    acc_ref[...] = jnp.zeros_like(acc_ref)
    neg_bound = -bound_ref[0, 0]

    def bounded_tile(j, carry):
        bias = tile_bias(j, neg_bound)
        for g in range(DSA_KV_HEADS):
            kt, vt = tile_operands(j, g)
            p = jnp.exp2(_dot(kt, qg_ref[g]) + bias)
            l_ref[g] += jnp.sum(p, axis=0, keepdims=True)
            acc_ref[g] += _dot(vt, p.astype(BF16))
        return carry

    _loop_by_pairs(na, lambda j: bounded_tile(j, 0))

    @pl.when(jnp.logical_not(jnp.min(l_ref[...]) >= DSA_MIN_DENOM))
    def _():
        m_ref[...] = jnp.full_like(m_ref, NEG_BIG)
        l_ref[...] = jnp.zeros_like(l_ref)
        acc_ref[...] = jnp.zeros_like(acc_ref)

        def online_tile(j, carry):
            bias = tile_bias(j, 0.0)
            for g in range(DSA_KV_HEADS):
                kt, vt = tile_operands(j, g)
                s = _dot(kt, qg_ref[g]) + bias
                m_old = m_ref[g]
                m_new = jnp.maximum(m_old, jnp.max(s, axis=0, keepdims=True))
                p = jnp.exp2(s - m_new)
                alpha = jnp.exp2(m_old - m_new)
                l_ref[g] = alpha * l_ref[g] + jnp.sum(p, axis=0, keepdims=True)
                m_ref[g] = m_new
                acc_ref[g] = alpha * acc_ref[g] + _dot(vt, p.astype(BF16))
            return carry

        lax.fori_loop(0, na, online_tile, 0)

    for g in range(DSA_KV_HEADS):
        o_g = acc_ref[g] * (1.0 / l_ref[g])
        for hh in range(grp):
            h = g * grp + hh
            o_ref[:, h * HEAD_DIM:(h + 1) * HEAD_DIM] = o_g[:, hh * TQ:(hh + 1) * TQ].T.astype(o_ref.dtype)


def _dsa(bound, iqt, iw, dqt, ik, dk, vt, batch, seq, topk):
    TQ = DSA_TQ
    nq = seq // TQ
    t = batch * seq
    nh = DSA_HEADS * HEAD_DIM
    nkv = DSA_KV_HEADS * HEAD_DIM
    grp = DSA_HEADS // DSA_KV_HEADS
    col = lambda b, i: (0, b * nq + i)
    return pl.pallas_call(
        functools.partial(_dsa_body, topk),
        grid=(batch, nq),
        in_specs=[
            pl.BlockSpec(memory_space=pltpu.SMEM),
            pl.BlockSpec((IDX_HEADS * IDX_DIM, TQ), col),
            pl.BlockSpec((TQ, LANES), lambda b, i: (b * nq + i, 0)),
            pl.BlockSpec((nh, TQ), col),
            pl.BlockSpec((seq, IDX_DIM), lambda b, i: (b, 0), pipeline_mode=pl.Buffered(1)),
            pl.BlockSpec((seq, nkv), lambda b, i: (b, 0), pipeline_mode=pl.Buffered(1)),
            pl.BlockSpec((nkv, seq), lambda b, i: (0, b), pipeline_mode=pl.Buffered(1)),
        ],
        out_specs=pl.BlockSpec((TQ, nh), lambda b, i: (b * nq + i, 0)),
        out_shape=jax.ShapeDtypeStruct((t, nh), BF16),
        scratch_shapes=[
            pltpu.VMEM((seq, TQ), jnp.int32),
            pltpu.VMEM((seq, TQ), jnp.int16),
            pltpu.VMEM((seq, TQ), jnp.int16),
            pltpu.VMEM((LANES, TQ), F32),
            pltpu.VMEM((DSA_KV_HEADS, HEAD_DIM, grp * TQ), BF16),
            pltpu.VMEM((DSA_KV_HEADS, 1, grp * TQ), F32),
            pltpu.VMEM((DSA_KV_HEADS, 1, grp * TQ), F32),
            pltpu.VMEM((DSA_KV_HEADS, HEAD_DIM, grp * TQ), F32),
        ],
        compiler_params=_params("parallel", "arbitrary"),
        name="dsa",
    )(bound, iqt, iw, dqt, ik, dk, vt)


def _merge_body(h_ref, og_ref, od_ref, wgg_ref, wgd_ref, wg_ref, wd_ref, bgg_ref, bgd_ref, o_ref):
    h = h_ref[...]
    sg = 1.0 / (1.0 + jnp.exp(-(_dot(h, wgg_ref[...]) + bgg_ref[...])))
    sd = 1.0 / (1.0 + jnp.exp(-(_dot(h, wgd_ref[...]) + bgd_ref[...])))
    yg = _dot(og_ref[...], wg_ref[...])
    yd = _dot(od_ref[...], wd_ref[...])
    o_ref[...] = (sg * yg + sd * yd).astype(o_ref.dtype)


def _merge(h, o_gla, o_dsa, w_gate, wg, wd, b_gate, tm=1024, tn=512):
    t, dh = h.shape
    dg = o_gla.shape[1]
    dd = o_dsa.shape[1]
    d = wg.shape[1]
    nb = d // tn
    return pl.pallas_call(
        _merge_body,
        grid=(t // tm, nb),
        in_specs=[
            pl.BlockSpec((tm, dh), lambda i, j: (i, 0)),
            pl.BlockSpec((tm, dg), lambda i, j: (i, 0)),
            pl.BlockSpec((tm, dd), lambda i, j: (i, 0)),
            pl.BlockSpec((dh, tn), lambda i, j: (0, j)),
            pl.BlockSpec((dh, tn), lambda i, j: (0, nb + j)),
            pl.BlockSpec((dg, tn), lambda i, j: (0, j)),
            pl.BlockSpec((dd, tn), lambda i, j: (0, j)),
            pl.BlockSpec((1, tn), lambda i, j: (0, j)),
            pl.BlockSpec((1, tn), lambda i, j: (0, nb + j)),
        ],
        out_specs=pl.BlockSpec((tm, tn), lambda i, j: (i, j)),
        out_shape=jax.ShapeDtypeStruct((t, d), BF16),
        compiler_params=_params("parallel", "parallel"),
        name="merge",
    )(h, o_gla, o_dsa, w_gate, w_gate, wg, wd, b_gate, b_gate)


def _outproj_body(x_ref, m_ref, w_ref, o_ref):
    o_ref[...] = x_ref[...] + _dot(m_ref[...], w_ref[...])


def _outproj(x2, m, w_out, tm=1024, tn=1024):
    t, d = x2.shape
    tn = min(tn, d)
    return pl.pallas_call(
        _outproj_body,
        grid=(t // tm, d // tn),
        in_specs=[
            pl.BlockSpec((tm, tn), lambda i, j: (i, j)),
            pl.BlockSpec((tm, d), lambda i, j: (i, 0)),
            pl.BlockSpec((d, tn), lambda i, j: (0, j)),
        ],
        out_specs=pl.BlockSpec((tm, tn), lambda i, j: (i, j)),
        out_shape=jax.ShapeDtypeStruct((t, d), F32),
        compiler_params=_params("parallel", "parallel"),
        name="out_proj",
    )(x2, m, w_out)


def _mlp_body(x_ref, g_ref, w1_ref, w2_ref, o_ref, h_scr):
    @pl.when(pl.program_id(1) == 0)
    def _():
        x = x_ref[...]
        h_scr[...] = (x * lax.rsqrt(jnp.mean(x * x, axis=-1, keepdims=True) + NORM_EPS) * g_ref[...]).astype(BF16)
        o_ref[...] = x

    u = jnp.maximum(_dot(h_scr[...], w1_ref[...]), 0.0)
    o_ref[...] += _dot((u * u).astype(BF16), w2_ref[...])


def _mlp(x1, g, w1, w2, tm=512, tf=1024):
    t, d = x1.shape
    ff = w1.shape[1]
    return pl.pallas_call(
        _mlp_body,
        grid=(t // tm, ff // tf),
        in_specs=[
            pl.BlockSpec((tm, d), lambda i, f: (i, 0)),
            pl.BlockSpec((1, d), lambda i, f: (0, 0)),
            pl.BlockSpec((d, tf), lambda i, f: (0, f)),
            pl.BlockSpec((tf, d), lambda i, f: (f, 0)),
        ],
        out_specs=pl.BlockSpec((tm, d), lambda i, f: (i, 0)),
        out_shape=jax.ShapeDtypeStruct((t, d), F32),
        scratch_shapes=[pltpu.VMEM((tm, d), BF16)],
        compiler_params=_params("parallel", "arbitrary"),
        name="mlp",
    )(x1, g.reshape(1, d), w1, w2)


def _rope_tables(seq, q_g, k_g, ik_g):
    pos = jnp.arange(seq, dtype=jnp.int32).astype(F32)

    def cs(rot):
        half = rot // 2
        inv = ROPE_THETA ** (-jnp.arange(half, dtype=F32) * 2.0 / rot)
        ang = pos[:, None] * inv[None, :]
        return jnp.cos(ang), jnp.sin(ang)

    c, s = cs(HEAD_DIM)
    hh = HEAD_DIM // 2
    ka = jnp.concatenate([c, c], axis=1) * k_g[None, :]
    kb = jnp.concatenate([-s * k_g[None, hh:], s * k_g[None, :hh]], axis=1)
    qs = (HEAD_DIM ** -0.5) * LOG2E
    qa = (jnp.concatenate([c, c], axis=1) * q_g[None, :] * qs).T
    qb = (jnp.concatenate([-s * q_g[None, hh:], s * q_g[None, :hh]], axis=1) * qs).T

    ci, si = cs(IDX_ROPE_DIM)
    h2 = IDX_ROPE_DIM // 2
    one = jnp.ones((seq, IDX_DIM - IDX_ROPE_DIM), F32)
    zero = jnp.zeros((seq, IDX_DIM - IDX_ROPE_DIM), F32)
    zh = jnp.zeros((seq, h2), F32)
    ia = jnp.concatenate([ci, ci, one], axis=1) * ik_g[None, :]
    ib1 = jnp.concatenate([-si * ik_g[None, h2:2 * h2], zh, zero], axis=1)
    ib2 = jnp.concatenate([zh, si * ik_g[None, :h2], zero], axis=1)
    iqs = IDX_DIM ** -0.5
    iqa = (jnp.concatenate([ci, ci, one], axis=1) * iqs).T
    iqb = (jnp.concatenate([-si, si, zero], axis=1) * iqs).T
    return (ka, kb, ia, ib1, ib2), (qa, qb), (iqa, iqb)


def kernel(x, norm1_g, w_in, gla_wg2, gla_bg, gla_norm_g, w_proj_gla, q_norm_g, k_norm_g, idx_k_norm_g,
           w_proj_dsa, b_gate, w_out, norm2_g, w_ff1, w_ff2):
    batch, seq, d = x.shape
    depth = w_in.shape[0]
    topk = min(TOPK_MAX, seq // 4)
    nqk = GLA_HEADS * GLA_DK
    nv = GLA_HEADS * GLA_DV
    ndq = DSA_HEADS * HEAD_DIM
    nkv = DSA_KV_HEADS * HEAD_DIM
    niq = IDX_HEADS * IDX_DIM
    splits = (nqk, nqk, nv, nv, GLA_GATE_RANK, ndq, nkv, nkv, niq, IDX_DIM, IDX_HEADS, 2 * d)
    offs = [0]
    for n in splits:
        offs.append(offs[-1] + n)
    o_gq, _, _, _, o_glr, o_dq, o_dk, o_dv, o_iq, o_ik, o_iw, o_gate, _ = offs
    assert (2 * nqk + 2 * nv) % 1024 == 0 and seq % 512 == 0 and d % 512 == 0

    x2 = x.reshape(batch * seq, d)
    for l in range(depth):
        wt = jnp.transpose(w_in[l])
        w_gla = _w_rows(wt, o_gq, o_glr - o_gq, True, "w_gla")
        w_gate = _w_rows(wt, o_gate, offs[-1] - o_gate, True, "w_gate")
        w_k = _w_key_side(wt, o_dk, nkv, o_ik, o_glr, o_iw)
        wt_dq = _w_rows(wt, o_dq, ndq, False, "wt_dq")
        wt_iq = _w_rows(wt, o_iq, niq, False, "wt_iq")
        wt_v = _w_rows(wt, o_dv, nkv, False, "wt_v")
        wg2p = jnp.concatenate(
            [gla_wg2[l], jnp.zeros((LANES - GLA_GATE_RANK, nqk), F32)], axis=0).astype(BF16)
        k_tabs, q_tabs, iq_tabs = _rope_tables(seq, q_norm_g[l], k_norm_g[l], idx_k_norm_g[l])

        h, ht = _rmsnorm_both(x2, norm1_g[l])
        p_gla = _mm(h, w_gla, F32, 1024, 1024, "proj_gla")
        dk, ik, glr, iw = _proj_k(h, w_k, k_tabs, seq)
        dqt = _proj_t("dq", wt_dq, ht, q_tabs, seq, BF16, 1024)
        iqt = _proj_t("iq", wt_iq, ht, iq_tabs, seq, BF16, 1024)
        vt = _proj_t("v", wt_v, ht, (), seq, BF16, nkv)

        b_cum, drop = _gla_decay(glr, wg2p, gla_bg[l].reshape(1, nqk))
        chunks_per_tile = batch * seq // GLA_CHUNK // drop.shape[0]
        mild = (drop[:, :chunks_per_tile, 0].reshape(-1) <= GLA_MAX_SPLIT_LOG2).astype(jnp.int32)
        o_gla = _gla(p_gla, b_cum, mild, gla_norm_g[l].reshape(1, GLA_DV), batch, seq)
        bound = (1.02 * HEAD_DIM * (HEAD_DIM ** -0.5) * LOG2E
                 * jnp.max(jnp.abs(q_norm_g[l])) * jnp.max(jnp.abs(k_norm_g[l]))).reshape(1, 1).astype(F32)
        o_dsa = _dsa(bound, iqt, iw, dqt, ik, dk, vt, batch, seq, topk)

        m = _merge(h, o_gla, o_dsa, w_gate, w_proj_gla[l].astype(BF16), w_proj_dsa[l].astype(BF16),
                   b_gate[l].reshape(1, 2 * d))
        x1 = _outproj(x2, m, w_out[l].astype(BF16))
        x2 = _mlp(x1, norm2_g[l], w_ff1[l].astype(BF16), w_ff2[l].astype(BF16))
    return x2.reshape(batch, seq, d)
```

```python
import functools
import math

import jax
import jax.numpy as jnp
from jax import lax
from jax.experimental import pallas as pl
from jax.experimental.pallas import tpu as pltpu

BF16 = jnp.bfloat16
F32 = jnp.float32

NORM_EPS = 1e-6
ROPE_THETA = 10000.0
GLA_HEADS = 4
GLA_DK = 256
GLA_DV = 512
GLA_GATE_RANK = 16
GLA_GATE_NORMALIZER = 16.0
DSA_HEADS = 16
DSA_KV_HEADS = 4
HEAD_DIM = 128
IDX_HEADS = 16
IDX_DIM = 128
IDX_ROPE_DIM = 64
TOPK_MAX = 256

LANES = 128
VMEM_LIMIT = 56 * 1024 * 1024
GLA_CHUNK = 128
GLA_SUB = 16
GLA_MAX_SPLIT_LOG2 = 100.0
DSA_TQ = 256
DSA_TK = 512
DSA_ATK = 512
DSA_COUNT_ROWS = 64
DSA_MIN_DENOM = 2.0 ** -80
INT_MIN = -(2 ** 31)
HALF_BIAS = 2 ** 15
NEG_BIG = -1e30
LOG2E = math.log2(math.e)


def _loop_by_pairs(n, body):
    def pair(jj, carry):
        body(2 * jj)
        body(2 * jj + 1)
        return carry

    lax.fori_loop(0, n // 2, pair, 0)

    @pl.when(n % 2 == 1)
    def _():
        body(n - 1)


def _params(*sem):
    return pltpu.CompilerParams(dimension_semantics=sem, vmem_limit_bytes=VMEM_LIMIT)


def _dot(a, b):
    return jnp.dot(a, b, preferred_element_type=F32)


def _rows_block(nrows, d, row0, step):
    return pl.BlockSpec((pl.Element(nrows), pl.Element(d)), lambda i: (pl.multiple_of(row0 + i * step, 8), 0))


def _wrows_body(to_natural, w_ref, o_ref):
    w = w_ref[...]
    o_ref[...] = (w.T if to_natural else w).astype(o_ref.dtype)


def _w_rows(wt, row0, nrows, to_natural, name, tr=512):
    d = wt.shape[1]
    tr = min(tr, nrows)
    out_spec = pl.BlockSpec((d, tr), lambda i: (0, i)) if to_natural else pl.BlockSpec((tr, d), lambda i: (i, 0))
    return pl.pallas_call(
        functools.partial(_wrows_body, to_natural),
        grid=(nrows // tr,),
        in_specs=[_rows_block(tr, d, row0, tr)],
        out_specs=out_spec,
        out_shape=jax.ShapeDtypeStruct((d, nrows) if to_natural else (nrows, d), BF16),
        compiler_params=_params("parallel"),
        name=name,
    )(wt)


def _wk_body(dk_ref, ik_ref, glr_ref, iw_ref, o_ref):
    nkv = dk_ref.shape[0]
    o_ref[:, :nkv] = dk_ref[...].T.astype(BF16)
    o_ref[:, nkv:nkv + IDX_DIM] = ik_ref[...].T.astype(BF16)

    def narrow(ref):
        pad = jnp.zeros((LANES - ref.shape[0], ref.shape[1]), F32)
        return jnp.concatenate([ref[...], pad], axis=0).T.astype(BF16)

    o_ref[:, nkv + IDX_DIM:nkv + IDX_DIM + LANES] = narrow(glr_ref)
    o_ref[:, nkv + IDX_DIM + LANES:] = narrow(iw_ref)


def _w_key_side(wt, o_dk, n_dk, o_ik, o_glr, o_iw):
    d = wt.shape[1]
    n_k = n_dk + IDX_DIM + 2 * LANES
    return pl.pallas_call(
        _wk_body,
        grid=(1,),
        in_specs=[_rows_block(n_dk, d, o_dk, 0), _rows_block(IDX_DIM, d, o_ik, 0),
                  _rows_block(GLA_GATE_RANK, d, o_glr, 0), _rows_block(IDX_HEADS, d, o_iw, 0)],
        out_specs=pl.BlockSpec((d, n_k), lambda i: (0, 0)),
        out_shape=jax.ShapeDtypeStruct((d, n_k), BF16),
        compiler_params=_params("arbitrary"),
        name="w_key_side",
    )(wt, wt, wt, wt)


def _rms_body(x_ref, g_ref, h_ref, ht_ref):
    x = x_ref[...]
    y = x * lax.rsqrt(jnp.mean(x * x, axis=-1, keepdims=True) + NORM_EPS) * g_ref[...]
    h_ref[...] = y.astype(BF16)
    ht_ref[...] = y.T.astype(BF16)


def _rmsnorm_both(x2, g, tm=256):
    t, d = x2.shape
    return pl.pallas_call(
        _rms_body,
        grid=(t // tm,),
        in_specs=[pl.BlockSpec((tm, d), lambda i: (i, 0)), pl.BlockSpec((1, d), lambda i: (0, 0))],
        out_specs=[pl.BlockSpec((tm, d), lambda i: (i, 0)), pl.BlockSpec((d, tm), lambda i: (0, i))],
        out_shape=[jax.ShapeDtypeStruct((t, d), BF16), jax.ShapeDtypeStruct((d, t), BF16)],
        compiler_params=_params("parallel"),
        name="rmsnorm_in",
    )(x2, g.reshape(1, d))


def _mm_body(a_ref, w_ref, o_ref):
    o_ref[...] = _dot(a_ref[...], w_ref[...]).astype(o_ref.dtype)


def _mm(a, w, out_dtype, tm, tn, name):
    m, k = a.shape
    n = w.shape[1]
    return pl.pallas_call(
        _mm_body,
        grid=(m // tm, n // tn),
        in_specs=[pl.BlockSpec((tm, k), lambda i, j: (i, 0)), pl.BlockSpec((k, tn), lambda i, j: (0, j))],
        out_specs=pl.BlockSpec((tm, tn), lambda i, j: (i, j)),
        out_shape=jax.ShapeDtypeStruct((m, n), out_dtype),
        compiler_params=_params("parallel", "parallel"),
        name=name,
    )(a, w)


def _projk_body(h_ref, w_ref, ka_ref, kb_ref, ia_ref, ib1_ref, ib2_ref, dk_ref, ik_ref, glr_ref, iw_ref):
    p = _dot(h_ref[...], w_ref[...])
    nkv = DSA_KV_HEADS
    for j in range(nkv):
        pj = p[:, j * HEAD_DIM:(j + 1) * HEAD_DIM]
        n = pj * lax.rsqrt(jnp.mean(pj * pj, axis=-1, keepdims=True) + NORM_EPS)
        out = n * ka_ref[...] + pltpu.roll(n, HEAD_DIM // 2, 1) * kb_ref[...]
        dk_ref[:, j * HEAD_DIM:(j + 1) * HEAD_DIM] = out.astype(BF16)
    c0 = nkv * HEAD_DIM
    pi = p[:, c0:c0 + IDX_DIM]
    n = pi * lax.rsqrt(jnp.mean(pi * pi, axis=-1, keepdims=True) + NORM_EPS)
    half = IDX_ROPE_DIM // 2
    out = n * ia_ref[...] + pltpu.roll(n, IDX_DIM - half, 1) * ib1_ref[...] + pltpu.roll(n, half, 1) * ib2_ref[...]
    ik_ref[...] = out.astype(BF16)
    c1 = c0 + IDX_DIM
    glr_ref[...] = p[:, c1:c1 + LANES]
    iw_ref[...] = p[:, c1 + LANES:] * (IDX_HEADS ** -0.5)


def _proj_k(h, w_k, tabs, seq, tm=512):
    t, d = h.shape
    nblk = seq // tm
    ncol = w_k.shape[1]
    tab_spec = pl.BlockSpec((tm, LANES), lambda i: (i % nblk, 0))
    return pl.pallas_call(
        _projk_body,
        grid=(t // tm,),
        in_specs=[pl.BlockSpec((tm, d), lambda i: (i, 0)), pl.BlockSpec((d, ncol), lambda i: (0, 0))] + [tab_spec] * 5,
        out_specs=[pl.BlockSpec((tm, DSA_KV_HEADS * HEAD_DIM), lambda i: (i, 0)),
                   pl.BlockSpec((tm, IDX_DIM), lambda i: (i, 0)),
                   pl.BlockSpec((tm, LANES), lambda i: (i, 0)),
                   pl.BlockSpec((tm, LANES), lambda i: (i, 0))],
        out_shape=[jax.ShapeDtypeStruct((t, DSA_KV_HEADS * HEAD_DIM), BF16),
                   jax.ShapeDtypeStruct((t, IDX_DIM), BF16),
                   jax.ShapeDtypeStruct((t, LANES), F32),
                   jax.ShapeDtypeStruct((t, LANES), F32)],
        compiler_params=_params("parallel"),
        name="proj_k",
    )(h, w_k, *tabs)


def _projt_body(kind, w_ref, ht_ref, *rest):
    p = _dot(w_ref[...], ht_ref[...])
    o_ref = rest[-1]
    rows = p.shape[0]
    if kind == "v":
        o_ref[...] = p.astype(o_ref.dtype)
        return
    ta_ref, tb_ref = rest[0], rest[1]
    for j in range(rows // HEAD_DIM):
        pj = p[j * HEAD_DIM:(j + 1) * HEAD_DIM]
        if kind == "dq":
            n = pj * lax.rsqrt(jnp.mean(pj * pj, axis=0, keepdims=True) + NORM_EPS)
            sw = jnp.concatenate([n[HEAD_DIM // 2:], n[:HEAD_DIM // 2]], axis=0)
        else:
            n = pj
            half = IDX_ROPE_DIM // 2
            sw = jnp.concatenate([n[half:2 * half], n[:half], n[2 * half:]], axis=0)
        o_ref[j * HEAD_DIM:(j + 1) * HEAD_DIM, :] = (n * ta_ref[...] + sw * tb_ref[...]).astype(o_ref.dtype)


def _proj_t(kind, w_t, ht, tabs, seq, out_dtype, tc, tt=512):
    c, d = w_t.shape
    t = ht.shape[1]
    nblk = seq // tt
    tab_spec = pl.BlockSpec((HEAD_DIM, tt), lambda i, j: (0, j % nblk))
    return pl.pallas_call(
        functools.partial(_projt_body, kind),
        grid=(c // tc, t // tt),
        in_specs=[pl.BlockSpec((tc, d), lambda i, j: (i, 0)), pl.BlockSpec((d, tt), lambda i, j: (0, j))]
        + [tab_spec] * len(tabs),
        out_specs=pl.BlockSpec((tc, tt), lambda i, j: (i, j)),
        out_shape=jax.ShapeDtypeStruct((c, t), out_dtype),
        compiler_params=_params("parallel", "parallel"),
        name="proj_t_" + kind,
    )(w_t, ht, *tabs)


def _gla_scores_explicit(q, k, b, bt):
    C, SB = GLA_CHUNK, GLA_SUB
    kt = k.T
    lane = lax.broadcasted_iota(jnp.int32, (SB, C), 1)
    rowi = lax.broadcasted_iota(jnp.int32, (SB, 1), 0)
    blocks = []
    for blk in range(C // SB):
        s = blk * SB
        q_i = q[s:s + SB]
        b_i = b[s:s + SB]
        a = jnp.zeros((SB, C), F32)
        for j in range(SB):
            r = s + j
            dec = jnp.where(rowi >= j, jnp.exp2(b_i - b[r:r + 1, :]), 0.0)
            col = jnp.sum(q_i * k[r:r + 1, :] * dec, axis=-1, keepdims=True)
            a = jnp.where(lane == r, col, a)
        if blk > 0:
            q_s = q_i * jnp.exp2(b_i - b[s - 1:s, :])
            k_s = kt * jnp.exp2(jnp.minimum(bt[:, s - 1:s] - bt, 0.0))
            a = jnp.where(lane < s, _dot(q_s.astype(BF16), k_s.astype(BF16)), a)
        blocks.append(a)
    return jnp.concatenate(blocks, axis=0)


def _gla_scores_split(q, k, b):
    C, SB = GLA_CHUNK, GLA_SUB
    nt = (((1,), (1,)), ((), ()))
    ri = lax.broadcasted_iota(jnp.int32, (C, C), 0)
    ci = lax.broadcasted_iota(jnp.int32, (C, C), 1)
    zeros = lambda n: jnp.zeros((n, GLA_DK), F32)
    a = jnp.zeros((C, C), F32)
    span = C // 2
    while span >= SB:
        q_parts, k_parts = [], []
        for lo in range(0, C, 2 * span):
            mid, hi = lo + span, lo + 2 * span
            ref = b[mid - 1:mid, :]
            k_parts += [k[lo:mid] * jnp.exp2(ref - b[lo:mid]), zeros(span)]
            q_parts += [zeros(span), q[mid:hi] * jnp.exp2(b[mid:hi] - ref)]
        s_l = lax.dot_general(jnp.concatenate(q_parts, axis=0).astype(BF16),
                              jnp.concatenate(k_parts, axis=0).astype(BF16), nt, preferred_element_type=F32)
        shift = (2 * span).bit_length() - 1
        a = a + jnp.where(jnp.right_shift(ri, shift) == jnp.right_shift(ci, shift), s_l, 0.0)
        span //= 2
    q_parts, k_parts = [], []
    for lo in range(0, C, SB):
        gap = b[lo:lo + SB] - b[lo - 1:lo, :] if lo > 0 else b[lo:lo + SB]
        q_parts.append(q[lo:lo + SB] * jnp.exp2(gap))
        k_parts.append(k[lo:lo + SB] * jnp.exp2(jnp.minimum(-gap, GLA_MAX_SPLIT_LOG2)))
    s_d = lax.dot_general(jnp.concatenate(q_parts, axis=0).astype(BF16),
                          jnp.concatenate(k_parts, axis=0).astype(BF16), nt, preferred_element_type=F32)
    shift = SB.bit_length() - 1
    same = jnp.right_shift(ri, shift) == jnp.right_shift(ci, shift)
    return a + jnp.where(same, jnp.where(ci <= ri, s_d, 0.0), 0.0)


def _gla_head(h, explicit, q_ref, k_ref, v_ref, gr_ref, ng_ref, o_ref, st_ref, b_ref):
    C = GLA_CHUNK
    qc = slice(h * GLA_DK, (h + 1) * GLA_DK)
    vc = slice(h * GLA_DV, (h + 1) * GLA_DV)
    q = q_ref[:, qc] * (GLA_DK ** -0.5)
    k = k_ref[:, qc]
    vb = v_ref[:, vc].astype(BF16)
    b = b_ref[:, qc]
    bt = b.T

    o = _dot((q * jnp.exp2(b)).astype(BF16), st_ref[h].astype(BF16))
    a = _gla_scores_explicit(q, k, b, bt) if explicit else _gla_scores_split(q, k, b)
    o = o + _dot(a.astype(BF16), vb)

    k_dec = k * jnp.exp2(b[C - 1:C, :] - b)
    st_ref[h] = st_ref[h] * jnp.exp2(bt[:, C - 1:C]) + _dot(k_dec.T.astype(BF16), vb)

    on = o * lax.rsqrt(jnp.mean(o * o, axis=-1, keepdims=True) + NORM_EPS) * ng_ref[...]
    gr = gr_ref[:, vc]
    o_ref[:, vc] = (on * gr * (1.0 / (1.0 + jnp.exp(-gr)))).astype(o_ref.dtype)


def _decay_body(glr_ref, wg2_ref, bg_ref, b_ref, drop_ref):
    C, SB = GLA_CHUNK, GLA_SUB
    z = _dot(glr_ref[...].astype(BF16), wg2_ref[...]) + bg_ref[...]
    g = (jnp.minimum(z, 0.0) - jnp.log1p(jnp.exp(-jnp.abs(z)))) * (LOG2E / GLA_GATE_NORMALIZER)
    ri = lax.broadcasted_iota(jnp.int32, (C, C), 0)
    ci = lax.broadcasted_iota(jnp.int32, (C, C), 1)
    tri = jnp.where(ci <= ri, 1.0, 0.0).astype(BF16)
    g1 = g.astype(BF16)
    r1 = g - g1.astype(F32)
    g2 = r1.astype(BF16)
    g3 = (r1 - g2.astype(F32)).astype(BF16)
    drop_ref[...] = jnp.zeros_like(drop_ref)
    for ch in range(glr_ref.shape[0] // C):
        rows = slice(ch * C, (ch + 1) * C)
        b = _dot(tri, g1[rows]) + _dot(tri, g2[rows]) + _dot(tri, g3[rows])
        b_ref[rows, :] = b
        drop = -b[SB - 1:SB, :]
        for blk in range(1, C // SB):
            s = blk * SB
            drop = jnp.maximum(drop, b[s - 1:s, :] - b[s + SB - 1:s + SB, :])
        drop_ref[0, ch:ch + 1, :] = jnp.broadcast_to(jnp.max(drop, axis=-1, keepdims=True), (1, LANES))


def _gla_decay(glr, wg2p, bg, tm=512):
    t = glr.shape[0]
    nqk = wg2p.shape[1]
    assert tm // GLA_CHUNK <= 8
    return pl.pallas_call(
        _decay_body,
        grid=(t // tm,),
        in_specs=[
            pl.BlockSpec((tm, LANES), lambda i: (i, 0)),
            pl.BlockSpec((LANES, nqk), lambda i: (0, 0)),
            pl.BlockSpec((1, nqk), lambda i: (0, 0)),
        ],
        out_specs=[pl.BlockSpec((tm, nqk), lambda i: (i, 0)), pl.BlockSpec((1, 8, LANES), lambda i: (i, 0, 0))],
        out_shape=[jax.ShapeDtypeStruct((t, nqk), F32), jax.ShapeDtypeStruct((t // tm, 8, LANES), F32)],
        compiler_params=_params("parallel"),
        name="gla_decay",
    )(glr, wg2p, bg)


def _gla_body(mild_ref, q_ref, k_ref, v_ref, gr_ref, b_ref, ng_ref, o_ref, st_ref):
    @pl.when(pl.program_id(1) == 0)
    def _():
        st_ref[...] = jnp.zeros_like(st_ref)

    refs = (q_ref, k_ref, v_ref, gr_ref, ng_ref, o_ref, st_ref, b_ref)
    mild = mild_ref[pl.program_id(0) * pl.num_programs(1) + pl.program_id(1)] == 1

    @pl.when(mild)
    def _():
        for h in range(GLA_HEADS):
            _gla_head(h, False, *refs)

    @pl.when(jnp.logical_not(mild))
    def _():
        for h in range(GLA_HEADS):
            _gla_head(h, True, *refs)


def _gla(p_gla, b_cum, mild, ng, batch, seq):
    C = GLA_CHUNK
    nc = seq // C
    t = batch * seq
    nqk = GLA_HEADS * GLA_DK
    nv = GLA_HEADS * GLA_DV
    vblk = 2 * nqk // nv
    row = lambda b, c: b * nc + c
    return pl.pallas_call(
        _gla_body,
        grid_spec=pltpu.PrefetchScalarGridSpec(
            num_scalar_prefetch=1,
            grid=(batch, nc),
            in_specs=[
                pl.BlockSpec((C, nqk), lambda b, c, m: (row(b, c), 0)),
                pl.BlockSpec((C, nqk), lambda b, c, m: (row(b, c), 1)),
                pl.BlockSpec((C, nv), lambda b, c, m: (row(b, c), vblk)),
                pl.BlockSpec((C, nv), lambda b, c, m: (row(b, c), vblk + 1)),
                pl.BlockSpec((C, nqk), lambda b, c, m: (row(b, c), 0)),
                pl.BlockSpec((1, GLA_DV), lambda b, c, m: (0, 0)),
            ],
            out_specs=pl.BlockSpec((C, nv), lambda b, c, m: (row(b, c), 0)),
            scratch_shapes=[pltpu.VMEM((GLA_HEADS, GLA_DK, GLA_DV), F32)],
        ),
        out_shape=jax.ShapeDtypeStruct((t, nv), BF16),
        compiler_params=_params("parallel", "arbitrary"),
        name="gla",
    )(mild, p_gla, p_gla, p_gla, p_gla, b_cum, ng)


def _dsa_body(topk, bound_ref, iqt_ref, iw_ref, dqt_ref, ik_ref, dk_ref, vt_ref, o_ref,
              sc_ref, hi_ref, lo_ref, iwt_ref, qg_ref, m_ref, l_ref, acc_ref):
    TQ, TK = DSA_TQ, DSA_TK
    i = pl.program_id(1)
    nk = (i * TQ + TQ + TK - 1) // TK
    q_pos = i * TQ + lax.broadcasted_iota(jnp.int32, (TK, TQ), 1)
    k_off = lax.broadcasted_iota(jnp.int32, (TK, TQ), 0)
    iwt_ref[...] = iw_ref[...].T

    def score_tile(j, carry):
        base = pl.multiple_of(j * TK, TK)
        ikt = ik_ref[pl.ds(base, TK), :]
        acc = jnp.zeros((TK, TQ), F32)
        for h in range(IDX_HEADS):
            r = _dot(ikt, iqt_ref[h * IDX_DIM:(h + 1) * IDX_DIM, :])
            acc = acc + jnp.maximum(r, 0.0) * iwt_ref[h:h + 1, :]
        bits = pltpu.bitcast(acc, jnp.int32)
        key = jnp.where(bits < 0, bits ^ jnp.int32(0x7FFFFFFF), bits)
        key = jnp.where(base + k_off <= q_pos, key, jnp.int32(INT_MIN))
        sc_ref[pl.ds(base, TK), :] = key
        hi_ref[pl.ds(base, TK), :] = jnp.right_shift(key, 16).astype(jnp.int16)
        lo_ref[pl.ds(base, TK), :] = ((key & jnp.int32(0xFFFF)) - jnp.int32(HALF_BIAS)).astype(jnp.int16)
        return carry

    _loop_by_pairs(nk, lambda j: score_tile(j, 0))

    CR = DSA_COUNT_ROWS

    def count_ge16(ref, cand16):
        def body(j, cnt):
            base = pl.multiple_of(j * TK, TK)
            hit = jnp.where(ref[pl.ds(base, TK), :] >= cand16, jnp.int16(1), jnp.int16(0))
            for r in range(TK // CR):
                cnt = cnt + hit[r * CR:(r + 1) * CR]
            return cnt
        cnt = lax.fori_loop(0, nk, body, jnp.zeros((CR, TQ), jnp.int16))
        return jnp.sum(cnt.astype(jnp.int32), axis=0, keepdims=True)

    def kth_largest16(ref, rank):
        zero = jnp.zeros((1, TQ), jnp.int32)
        c0 = count_ge16(ref, zero.astype(jnp.int16))
        ok0 = c0 >= rank
        state0 = (jnp.where(ok0, zero, jnp.int32(-HALF_BIAS)), jnp.where(ok0, c0, nk * TK))

        def bit_step(it, state):
            t, cnt = state
            cand = t | jnp.left_shift(jnp.int32(1), 14 - it)
            c = count_ge16(ref, cand.astype(jnp.int16))
            ok = c >= rank
            return jnp.where(ok, cand, t), jnp.where(ok, c, cnt)

        return lax.fori_loop(0, 15, bit_step, state0)

    def count_above16(ref, t):
        c = count_ge16(ref, jnp.minimum(t + 1, HALF_BIAS - 1).astype(jnp.int16))
        return jnp.where(t >= HALF_BIAS - 1, 0, c)

    t_hi, n_hi = kth_largest16(hi_ref, topk)
    above = count_above16(hi_ref, t_hi)
    t_hi16 = t_hi.astype(jnp.int16)

    def low_tile(j, carry):
        base = pl.multiple_of(j * TK, TK)
        rows = pl.ds(base, TK)
        lo_ref[rows, :] = jnp.where(hi_ref[rows, :] == t_hi16, lo_ref[rows, :], jnp.int16(-HALF_BIAS))
        return carry

    lax.fori_loop(0, nk, low_tile, 0)
    t_lo, n_lo = kth_largest16(lo_ref, topk - above)
    thr = jnp.left_shift(t_hi, 16) + (t_lo + HALF_BIAS)

    n_lo = jnp.where(t_lo == -HALF_BIAS, n_hi - above, n_lo)
    n_adm = i * TQ + 1 + lax.broadcasted_iota(jnp.int32, (1, TQ), 1)
    surplus = jnp.where(n_adm > topk, above + n_lo - topk, 0)

    @pl.when(jnp.max(surplus) > 0)
    def _():
        quota = jnp.where(surplus > 0, topk - above - count_above16(lo_ref, t_lo), nk * TK).astype(F32)
        before = (lax.broadcasted_iota(jnp.int32, (TK, TK), 1)
                  < lax.broadcasted_iota(jnp.int32, (TK, TK), 0))
        before = jnp.where(before, 1.0, 0.0).astype(BF16)

        def tie_tile(j, seen):
            base = pl.multiple_of(j * TK, TK)
            key = sc_ref[pl.ds(base, TK), :]
            tie = jnp.where(key == thr, 1.0, 0.0)
            rank = seen + _dot(before, tie.astype(BF16))
            sc_ref[pl.ds(base, TK), :] = jnp.where((tie > 0.0) & (rank >= quota), jnp.int32(INT_MIN), key)
            return seen + jnp.sum(tie, axis=0, keepdims=True)

        lax.fori_loop(0, nk, tie_tile, jnp.zeros((1, TQ), F32))

    thr = jnp.maximum(thr, jnp.int32(INT_MIN + 1))

    grp = DSA_HEADS // DSA_KV_HEADS
    AK = DSA_ATK
    na = (i * TQ + TQ + AK - 1) // AK
    for g in range(DSA_KV_HEADS):
        qg_ref[g] = jnp.concatenate(
            [dqt_ref[(g * grp + hh) * HEAD_DIM:(g * grp + hh + 1) * HEAD_DIM, :] for hh in range(grp)], axis=1)

    def tile_operands(j, g):
        base = pl.multiple_of(j * AK, AK)
        kt = dk_ref[pl.ds(base, AK), g * HEAD_DIM:(g + 1) * HEAD_DIM]
        vt = vt_ref[g * HEAD_DIM:(g + 1) * HEAD_DIM, pl.ds(base, AK)]
        return kt, vt

    def tile_bias(j, selected):
        base = pl.multiple_of(j * AK, AK)
        bias = jnp.where(sc_ref[pl.ds(base, AK), :] >= thr, selected, NEG_BIG)
        return jnp.concatenate([bias] * grp, axis=1)

    l_ref[...] = jnp.zeros_like(l_ref)
    acc_ref[...] = jnp.zeros_like(acc_ref)
    neg_bound = -bound_ref[0, 0]

    def bounded_tile(j, carry):
        bias = tile_bias(j, neg_bound)
        for g in range(DSA_KV_HEADS):
            kt, vt = tile_operands(j, g)
            p = jnp.exp2(_dot(kt, qg_ref[g]) + bias)
            l_ref[g] += jnp.sum(p, axis=0, keepdims=True)
            acc_ref[g] += _dot(vt, p.astype(BF16))
        return carry

    _loop_by_pairs(na, lambda j: bounded_tile(j, 0))

    @pl.when(jnp.logical_not(jnp.min(l_ref[...]) >= DSA_MIN_DENOM))
    def _():
        m_ref[...] = jnp.full_like(m_ref, NEG_BIG)
        l_ref[...] = jnp.zeros_like(l_ref)
        acc_ref[...] = jnp.zeros_like(acc_ref)

        def online_tile(j, carry):
            bias = tile_bias(j, 0.0)
            for g in range(DSA_KV_HEADS):
                kt, vt = tile_operands(j, g)
                s = _dot(kt, qg_ref[g]) + bias
                m_old = m_ref[g]
                m_new = jnp.maximum(m_old, jnp.max(s, axis=0, keepdims=True))
                p = jnp.exp2(s - m_new)
                alpha = jnp.exp2(m_old - m_new)
                l_ref[g] = alpha * l_ref[g] + jnp.sum(p, axis=0, keepdims=True)
                m_ref[g] = m_new
                acc_ref[g] = alpha * acc_ref[g] + _dot(vt, p.astype(BF16))
            return carry

        lax.fori_loop(0, na, online_tile, 0)

    for g in range(DSA_KV_HEADS):
        o_g = acc_ref[g] * (1.0 / l_ref[g])
        for hh in range(grp):
            h = g * grp + hh
            o_ref[:, h * HEAD_DIM:(h + 1) * HEAD_DIM] = o_g[:, hh * TQ:(hh + 1) * TQ].T.astype(o_ref.dtype)


def _dsa(bound, iqt, iw, dqt, ik, dk, vt, batch, seq, topk):
    TQ = DSA_TQ
    nq = seq // TQ
    t = batch * seq
    nh = DSA_HEADS * HEAD_DIM
    nkv = DSA_KV_HEADS * HEAD_DIM
    grp = DSA_HEADS // DSA_KV_HEADS
    col = lambda b, i: (0, b * nq + i)
    return pl.pallas_call(
        functools.partial(_dsa_body, topk),
        grid=(batch, nq),
        in_specs=[
            pl.BlockSpec(memory_space=pltpu.SMEM),
            pl.BlockSpec((IDX_HEADS * IDX_DIM, TQ), col),
            pl.BlockSpec((TQ, LANES), lambda b, i: (b * nq + i, 0)),
            pl.BlockSpec((nh, TQ), col),
            pl.BlockSpec((seq, IDX_DIM), lambda b, i: (b, 0), pipeline_mode=pl.Buffered(1)),
            pl.BlockSpec((seq, nkv), lambda b, i: (b, 0), pipeline_mode=pl.Buffered(1)),
            pl.BlockSpec((nkv, seq), lambda b, i: (0, b), pipeline_mode=pl.Buffered(1)),
        ],
        out_specs=pl.BlockSpec((TQ, nh), lambda b, i: (b * nq + i, 0)),
        out_shape=jax.ShapeDtypeStruct((t, nh), BF16),
        scratch_shapes=[
            pltpu.VMEM((seq, TQ), jnp.int32),
            pltpu.VMEM((seq, TQ), jnp.int16),
            pltpu.VMEM((seq, TQ), jnp.int16),
            pltpu.VMEM((LANES, TQ), F32),
            pltpu.VMEM((DSA_KV_HEADS, HEAD_DIM, grp * TQ), BF16),
            pltpu.VMEM((DSA_KV_HEADS, 1, grp * TQ), F32),
            pltpu.VMEM((DSA_KV_HEADS, 1, grp * TQ), F32),
            pltpu.VMEM((DSA_KV_HEADS, HEAD_DIM, grp * TQ), F32),
        ],
        compiler_params=_params("parallel", "arbitrary"),
        name="dsa",
    )(bound, iqt, iw, dqt, ik, dk, vt)


def _merge_body(h_ref, og_ref, od_ref, wgg_ref, wgd_ref, wg_ref, wd_ref, bgg_ref, bgd_ref, o_ref):
    h = h_ref[...]
    sg = 1.0 / (1.0 + jnp.exp(-(_dot(h, wgg_ref[...]) + bgg_ref[...])))
    sd = 1.0 / (1.0 + jnp.exp(-(_dot(h, wgd_ref[...]) + bgd_ref[...])))
    yg = _dot(og_ref[...], wg_ref[...])
    yd = _dot(od_ref[...], wd_ref[...])
    o_ref[...] = (sg * yg + sd * yd).astype(o_ref.dtype)


def _merge(h, o_gla, o_dsa, w_gate, wg, wd, b_gate, tm=1024, tn=512):
    t, dh = h.shape
    dg = o_gla.shape[1]
    dd = o_dsa.shape[1]
    d = wg.shape[1]
    nb = d // tn
    return pl.pallas_call(
        _merge_body,
        grid=(t // tm, nb),
        in_specs=[
            pl.BlockSpec((tm, dh), lambda i, j: (i, 0)),
            pl.BlockSpec((tm, dg), lambda i, j: (i, 0)),
            pl.BlockSpec((tm, dd), lambda i, j: (i, 0)),
            pl.BlockSpec((dh, tn), lambda i, j: (0, j)),
            pl.BlockSpec((dh, tn), lambda i, j: (0, nb + j)),
            pl.BlockSpec((dg, tn), lambda i, j: (0, j)),
            pl.BlockSpec((dd, tn), lambda i, j: (0, j)),
            pl.BlockSpec((1, tn), lambda i, j: (0, j)),
            pl.BlockSpec((1, tn), lambda i, j: (0, nb + j)),
        ],
        out_specs=pl.BlockSpec((tm, tn), lambda i, j: (i, j)),
        out_shape=jax.ShapeDtypeStruct((t, d), BF16),
        compiler_params=_params("parallel", "parallel"),
        name="merge",
    )(h, o_gla, o_dsa, w_gate, w_gate, wg, wd, b_gate, b_gate)


def _merge_out_body(x_ref, h_ref, og_ref, od_ref, wgg_ref, wgd_ref, wg_ref, wd_ref, bgg_ref, bgd_ref, wo_ref, o_ref):
    @pl.when(pl.program_id(1) == 0)
    def _():
        o_ref[...] = x_ref[...]

    h = h_ref[...]
    sg = 1.0 / (1.0 + jnp.exp(-(_dot(h, wgg_ref[...]) + bgg_ref[...])))
    sd = 1.0 / (1.0 + jnp.exp(-(_dot(h, wgd_ref[...]) + bgd_ref[...])))
    m = sg * _dot(og_ref[...], wg_ref[...]) + sd * _dot(od_ref[...], wd_ref[...])
    o_ref[...] += _dot(m.astype(BF16), wo_ref[...])


def _merge_outproj(x2, h, o_gla, o_dsa, w_gate, wg, wd, b_gate, w_out, tm=512, tn=512):
    t, dh = h.shape
    dg = o_gla.shape[1]
    dd = o_dsa.shape[1]
    d = wg.shape[1]
    nb = d // tn
    return pl.pallas_call(
        _merge_out_body,
        grid=(t // tm, nb),
        in_specs=[
            pl.BlockSpec((tm, d), lambda i, j: (i, 0)),
            pl.BlockSpec((tm, dh), lambda i, j: (i, 0)),
            pl.BlockSpec((tm, dg), lambda i, j: (i, 0)),
            pl.BlockSpec((tm, dd), lambda i, j: (i, 0)),
            pl.BlockSpec((dh, tn), lambda i, j: (0, j)),
            pl.BlockSpec((dh, tn), lambda i, j: (0, nb + j)),
            pl.BlockSpec((dg, tn), lambda i, j: (0, j)),
            pl.BlockSpec((dd, tn), lambda i, j: (0, j)),
            pl.BlockSpec((1, tn), lambda i, j: (0, j)),
            pl.BlockSpec((1, tn), lambda i, j: (0, nb + j)),
            pl.BlockSpec((tn, d), lambda i, j: (j, 0)),
        ],
        out_specs=pl.BlockSpec((tm, d), lambda i, j: (i, 0)),
        out_shape=jax.ShapeDtypeStruct((t, d), F32),
        compiler_params=_params("parallel", "arbitrary"),
        name="merge_out",
    )(x2, h, o_gla, o_dsa, w_gate, w_gate, wg, wd, b_gate, b_gate, w_out)


def _outproj_body(x_ref, m_ref, w_ref, o_ref):
    o_ref[...] = x_ref[...] + _dot(m_ref[...], w_ref[...])


def _outproj(x2, m, w_out, tm=1024, tn=1024):
    t, d = x2.shape
    tn = min(tn, d)
    return pl.pallas_call(
        _outproj_body,
        grid=(t // tm, d // tn),
        in_specs=[
            pl.BlockSpec((tm, tn), lambda i, j: (i, j)),
            pl.BlockSpec((tm, d), lambda i, j: (i, 0)),
            pl.BlockSpec((d, tn), lambda i, j: (0, j)),
        ],
        out_specs=pl.BlockSpec((tm, tn), lambda i, j: (i, j)),
        out_shape=jax.ShapeDtypeStruct((t, d), F32),
        compiler_params=_params("parallel", "parallel"),
        name="out_proj",
    )(x2, m, w_out)


def _mlp_body(x_ref, g_ref, w1_ref, w2_ref, o_ref, h_scr):
    @pl.when(pl.program_id(1) == 0)
    def _():
        x = x_ref[...]
        h_scr[...] = (x * lax.rsqrt(jnp.mean(x * x, axis=-1, keepdims=True) + NORM_EPS) * g_ref[...]).astype(BF16)
        o_ref[...] = x

    u = jnp.maximum(_dot(h_scr[...], w1_ref[...]), 0.0)
    o_ref[...] += _dot((u * u).astype(BF16), w2_ref[...])


def _mlp(x1, g, w1, w2, tm=512, tf=1024):
    t, d = x1.shape
    ff = w1.shape[1]
    return pl.pallas_call(
        _mlp_body,
        grid=(t // tm, ff // tf),
        in_specs=[
            pl.BlockSpec((tm, d), lambda i, f: (i, 0)),
            pl.BlockSpec((1, d), lambda i, f: (0, 0)),
            pl.BlockSpec((d, tf), lambda i, f: (0, f)),
            pl.BlockSpec((tf, d), lambda i, f: (f, 0)),
        ],
        out_specs=pl.BlockSpec((tm, d), lambda i, f: (i, 0)),
        out_shape=jax.ShapeDtypeStruct((t, d), F32),
        scratch_shapes=[pltpu.VMEM((tm, d), BF16)],
        compiler_params=_params("parallel", "arbitrary"),
        name="mlp",
    )(x1, g.reshape(1, d), w1, w2)


def _rope_tables(seq, q_g, k_g, ik_g):
    pos = jnp.arange(seq, dtype=jnp.int32).astype(F32)

    def cs(rot):
        half = rot // 2
        inv = ROPE_THETA ** (-jnp.arange(half, dtype=F32) * 2.0 / rot)
        ang = pos[:, None] * inv[None, :]
        return jnp.cos(ang), jnp.sin(ang)

    c, s = cs(HEAD_DIM)
    hh = HEAD_DIM // 2
    ka = jnp.concatenate([c, c], axis=1) * k_g[None, :]
    kb = jnp.concatenate([-s * k_g[None, hh:], s * k_g[None, :hh]], axis=1)
    qs = (HEAD_DIM ** -0.5) * LOG2E
    qa = (jnp.concatenate([c, c], axis=1) * q_g[None, :] * qs).T
    qb = (jnp.concatenate([-s * q_g[None, hh:], s * q_g[None, :hh]], axis=1) * qs).T

    ci, si = cs(IDX_ROPE_DIM)
    h2 = IDX_ROPE_DIM // 2
    one = jnp.ones((seq, IDX_DIM - IDX_ROPE_DIM), F32)
    zero = jnp.zeros((seq, IDX_DIM - IDX_ROPE_DIM), F32)
    zh = jnp.zeros((seq, h2), F32)
    ia = jnp.concatenate([ci, ci, one], axis=1) * ik_g[None, :]
    ib1 = jnp.concatenate([-si * ik_g[None, h2:2 * h2], zh, zero], axis=1)
    ib2 = jnp.concatenate([zh, si * ik_g[None, :h2], zero], axis=1)
    iqs = IDX_DIM ** -0.5
    iqa = (jnp.concatenate([ci, ci, one], axis=1) * iqs).T
    iqb = (jnp.concatenate([-si, si, zero], axis=1) * iqs).T
    return (ka, kb, ia, ib1, ib2), (qa, qb), (iqa, iqb)


def kernel(x, norm1_g, w_in, gla_wg2, gla_bg, gla_norm_g, w_proj_gla, q_norm_g, k_norm_g, idx_k_norm_g,
           w_proj_dsa, b_gate, w_out, norm2_g, w_ff1, w_ff2):
    batch, seq, d = x.shape
    depth = w_in.shape[0]
    topk = min(TOPK_MAX, seq // 4)
    nqk = GLA_HEADS * GLA_DK
    nv = GLA_HEADS * GLA_DV
    ndq = DSA_HEADS * HEAD_DIM
    nkv = DSA_KV_HEADS * HEAD_DIM
    niq = IDX_HEADS * IDX_DIM
    splits = (nqk, nqk, nv, nv, GLA_GATE_RANK, ndq, nkv, nkv, niq, IDX_DIM, IDX_HEADS, 2 * d)
    offs = [0]
    for n in splits:
        offs.append(offs[-1] + n)
    o_gq, _, _, _, o_glr, o_dq, o_dk, o_dv, o_iq, o_ik, o_iw, o_gate, _ = offs
    assert (2 * nqk + 2 * nv) % 1024 == 0 and seq % 512 == 0 and d % 512 == 0

    x2 = x.reshape(batch * seq, d)
    for l in range(depth):
        wt = jnp.transpose(w_in[l])
        w_gla = _w_rows(wt, o_gq, o_glr - o_gq, True, "w_gla")
        w_gate = _w_rows(wt, o_gate, offs[-1] - o_gate, True, "w_gate")
        w_k = _w_key_side(wt, o_dk, nkv, o_ik, o_glr, o_iw)
        wt_dq = _w_rows(wt, o_dq, ndq, False, "wt_dq")
        wt_iq = _w_rows(wt, o_iq, niq, False, "wt_iq")
        wt_v = _w_rows(wt, o_dv, nkv, False, "wt_v")
        wg2p = jnp.concatenate(
            [gla_wg2[l], jnp.zeros((LANES - GLA_GATE_RANK, nqk), F32)], axis=0).astype(BF16)
        k_tabs, q_tabs, iq_tabs = _rope_tables(seq, q_norm_g[l], k_norm_g[l], idx_k_norm_g[l])

        h, ht = _rmsnorm_both(x2, norm1_g[l])
        p_gla = _mm(h, w_gla, F32, 1024, 1024, "proj_gla")
        dk, ik, glr, iw = _proj_k(h, w_k, k_tabs, seq)
        dqt = _proj_t("dq", wt_dq, ht, q_tabs, seq, BF16, 1024)
        iqt = _proj_t("iq", wt_iq, ht, iq_tabs, seq, BF16, 1024)
        vt = _proj_t("v", wt_v, ht, (), seq, BF16, nkv)

        b_cum, drop = _gla_decay(glr, wg2p, gla_bg[l].reshape(1, nqk))
        chunks_per_tile = batch * seq // GLA_CHUNK // drop.shape[0]
        mild = (drop[:, :chunks_per_tile, 0].reshape(-1) <= GLA_MAX_SPLIT_LOG2).astype(jnp.int32)
        o_gla = _gla(p_gla, b_cum, mild, gla_norm_g[l].reshape(1, GLA_DV), batch, seq)
        bound = (1.02 * HEAD_DIM * (HEAD_DIM ** -0.5) * LOG2E
                 * jnp.max(jnp.abs(q_norm_g[l])) * jnp.max(jnp.abs(k_norm_g[l]))).reshape(1, 1).astype(F32)
        o_dsa = _dsa(bound, iqt, iw, dqt, ik, dk, vt, batch, seq, topk)

        x1 = _merge_outproj(x2, h, o_gla, o_dsa, w_gate, w_proj_gla[l].astype(BF16), w_proj_dsa[l].astype(BF16),
                            b_gate[l].reshape(1, 2 * d), w_out[l].astype(BF16))
        x2 = _mlp(x1, norm2_g[l], w_ff1[l].astype(BF16), w_ff2[l].astype(BF16))
    return x2.reshape(batch, seq, d)
```
